```python
import math
import jax, jax.numpy as jnp
from jax import lax
import numpy as np

D_MODEL = 1024
BATCH = 8
SEQ = 4096
DEPTH = 2
DEC_BATCH = 32
DEC_SEQ = 4
PAST_LEN = 16384
PAGE_SIZE = 128

SB_HEADS = 8
SB_DH = 64
SB_WIDTH = SB_HEADS * SB_DH
DF_HEADS = 8
DF_KV_HEADS = 4
DF_GROUP = DF_HEADS // DF_KV_HEADS
DF_DH = 64
DF_VDIM = 2 * DF_DH
DF_WIDTH = DF_HEADS * DF_VDIM
ML_HEADS = 4
ML_DH = 128
ML_WIDTH = ML_HEADS * ML_DH
ML_CONV = 4
ML_CHUNK = 128
N_BRANCH = 3
PK_HEADS = 8
PK_NKEYS = 128
PK_EXPERTS = PK_NKEYS * PK_NKEYS
PK_QDIM = 256
PK_HALF = PK_QDIM // 2
PK_TOPK = 16
PK_BLOCK = 128
Q_BLOCK = 128
EPS = 1e-6
IN_SPLITS = (SB_WIDTH, SB_WIDTH, SB_WIDTH,
             DF_HEADS * 2 * DF_DH, DF_KV_HEADS * 2 * DF_DH, DF_KV_HEADS * DF_VDIM,
             2 * ML_WIDTH, ML_WIDTH, ML_WIDTH, ML_HEADS, ML_HEADS,
             N_BRANCH * D_MODEL)
IN_WIDTH = sum(IN_SPLITS)

kernel_name = 'hybrid_sb_diff_mlstm_peer_step'


def rmsnorm(x, g):
    xf = x.astype(jnp.float32)
    y = xf * lax.rsqrt(jnp.mean(xf * xf, axis=-1, keepdims=True) + EPS)
    return (y * g.astype(jnp.float32)).astype(x.dtype)


def alibi_slopes(n):
    return jnp.array([2.0 ** (-8.0 * (h + 1) / n) for h in range(n)], dtype=jnp.float32)


def adaln(c, w, b):
    mod = (jax.nn.silu(c) @ w + b)[:, None, :]
    return jnp.split(mod, 6, axis=-1)


def modulate(x, g, shift, scale):
    return rmsnorm(x, g) * (1.0 + scale) + shift


def project_inputs(h, w_in):
    B, T, _ = h.shape
    p = h @ w_in
    parts, start = [], 0
    for size in IN_SPLITS:
        parts.append(p[..., start:start + size])
        start += size
    sb_q, sb_k, sb_v, df_q, df_k, df_v, ml_qk, ml_v, ml_o, ml_i, ml_f, gates = parts
    return (sb_q.reshape(B, T, SB_HEADS, SB_DH), sb_k.reshape(B, T, SB_HEADS, SB_DH),
            sb_v.reshape(B, T, SB_HEADS, SB_DH),
            df_q.reshape(B, T, DF_KV_HEADS, DF_GROUP, 2 * DF_DH),
            df_k.reshape(B, T, DF_KV_HEADS, 2 * DF_DH), df_v.reshape(B, T, DF_KV_HEADS, DF_VDIM),
            ml_qk, ml_v, ml_o, ml_i, ml_f, gates)


def sb_attend(q, k, v, q_pos, k_pos):
    z = jnp.einsum('bqhd,bkhd->bhqk', q, k).astype(jnp.float32) * (SB_DH ** -0.5)
    earlier = k_pos[None, :] < q_pos[:, None]
    log_keep = jnp.where(earlier, jax.nn.log_sigmoid(-z), 0.0)
    between = lax.cumsum(log_keep, axis=z.ndim - 1, reverse=True) - log_keep
    w = jnp.where(earlier, jnp.exp(jax.nn.log_sigmoid(z) + between), 0.0)
    return jnp.einsum('bhqk,bkhd->bqhd', w.astype(v.dtype), v)


def diff_attend(q, k, v, q_pos, k_pos, lam, slopes):
    dist = (q_pos[:, None] - k_pos[None, :]).astype(jnp.float32)
    visible = dist >= 0
    bias = -slopes[:, :, None, None] * dist

    def attn_probs(qa, ka):
        s = jnp.einsum('bqkgd,bskd->bkgqs', qa, ka).astype(jnp.float32) * (DF_DH ** -0.5) + bias
        return jax.nn.softmax(jnp.where(visible, s, -jnp.inf), axis=-1)

    p = attn_probs(q[..., :DF_DH], k[..., :DF_DH]) - lam * attn_probs(q[..., DF_DH:], k[..., DF_DH:])
    return jnp.einsum('bkgqs,bskd->bqkgd', p.astype(v.dtype), v)


def query_blocks(attend, q):
    n_blocks = q.shape[1] // Q_BLOCK

    def body(i):
        start = i * Q_BLOCK
        qb = lax.dynamic_slice_in_dim(q, start, Q_BLOCK, axis=1)
        return attend(qb, start + jnp.arange(Q_BLOCK, dtype=jnp.int32))

    out = jnp.moveaxis(lax.map(body, jnp.arange(n_blocks, dtype=jnp.int32)), 0, 1)
    return out.reshape((out.shape[0], n_blocks * Q_BLOCK) + out.shape[3:])


def sample_attention(layer, sb_q, sb_k, sb_v, df_q, df_k, df_v, page_table,
                     cache_sb_k, cache_sb_v, cache_df_k, cache_df_v, lam, slopes, q_pos, k_pos):
    def with_past(cache, pages, new):
        past = cache[layer, pages]
        past = past.reshape((-1,) + past.shape[2:])
        return jnp.concatenate([past, new.astype(past.dtype)], axis=0)[None]

    def one_sequence(args):
        sq, sk, sv, dq, dk, dv, pages = args
        sb = sb_attend(sq[None], with_past(cache_sb_k, pages, sk), with_past(cache_sb_v, pages, sv),
                       q_pos, k_pos)
        df = diff_attend(dq[None], with_past(cache_df_k, pages, dk), with_past(cache_df_v, pages, dv),
                         q_pos, k_pos, lam, slopes)
        return sb[0], df[0]

    return lax.map(one_sequence, (sb_q, sb_k, sb_v, df_q, df_k, df_v, page_table))


def mlstm_inputs(ml_qk, ml_v, ml_i, ml_f, conv_buf, w_conv, b_conv, b_ig, b_fg):
    B, T, _ = ml_qk.shape
    window = jnp.concatenate([conv_buf.astype(ml_qk.dtype), ml_qk], axis=1)
    qk = jax.nn.silu(sum(window[:, j:j + T] * w_conv[j] for j in range(ML_CONV)) + b_conv)

    def heads(a):
        return a.reshape(B, T, ML_HEADS, ML_DH).transpose(0, 2, 1, 3)

    q = heads(qk[..., :ML_WIDTH])
    k = heads(qk[..., ML_WIDTH:]) * (ML_DH ** -0.5)
    v = heads(ml_v)
    ig = (ml_i.astype(jnp.float32) + b_ig).transpose(0, 2, 1)
    lf = jax.nn.log_sigmoid(ml_f.astype(jnp.float32) + b_fg).transpose(0, 2, 1)
    return q, k, v, ig, lf, window[:, T:]


def mlstm_chunk(carry, inputs):
    c_mat, n_vec, m_run = (a.astype(jnp.float32) for a in carry)
    q, k, v, ig, lf = inputs
    q, k, v = q.astype(jnp.float32), k.astype(jnp.float32), v.astype(jnp.float32)
    L = q.shape[2]
    b = jnp.cumsum(lf, axis=-1)
    causal = jnp.tril(jnp.ones((L, L), dtype=bool))
    log_d = jnp.where(causal, b[..., :, None] - b[..., None, :] + ig[..., None, :], -jnp.inf)
    log_inter = b + m_run[..., None]
    m_t = jnp.maximum(log_inter, jnp.max(log_d, axis=-1))
    a_inter = jnp.exp(log_inter - m_t)
    scores = jnp.einsum('bhtd,bhsd->bhts', q, k) * jnp.exp(log_d - m_t[..., None])
    num = (a_inter[..., None] * jnp.einsum('bhtd,bhde->bhte', q, c_mat)
           + jnp.einsum('bhts,bhse->bhte', scores, v))
    den = a_inter * jnp.einsum('bhtd,bhd->bht', q, n_vec) + jnp.sum(scores, axis=-1)
    h = num / jnp.maximum(jnp.abs(den), jnp.exp(-m_t))[..., None]
    m_new = m_t[..., -1]
    w_end = jnp.exp(b[..., -1:] - b + ig - m_new[..., None])
    decay = jnp.exp(b[..., -1] + m_run - m_new)
    c_new = decay[..., None, None] * c_mat + jnp.einsum('bhs,bhsd,bhse->bhde', w_end, k, v)
    n_new = decay[..., None] * n_vec + jnp.einsum('bhs,bhsd->bhd', w_end, k)
    return (c_new, n_new, m_new), h


def mlstm_prompt(q, k, v, ig, lf):
    B, H, T, d = q.shape
    nc = T // ML_CHUNK

    def chunks(a):
        return jnp.moveaxis(a.reshape((B, H, nc, ML_CHUNK) + a.shape[3:]), 2, 0)

    init = (jnp.zeros((B, H, d, d), jnp.float32), jnp.zeros((B, H, d), jnp.float32),
            jnp.zeros((B, H), jnp.float32))
    state, h = lax.scan(mlstm_chunk, init, (chunks(q), chunks(k), chunks(v), chunks(ig), chunks(lf)))
    return state, jnp.moveaxis(h, 0, 2).reshape(B, H, T, d)


def merge_branches(sb_o, df_o, ml_h, ml_o, gates, g_diff, lam_init, g_ml,
                   w_br_sb, w_br_df, w_br_ml, w_out):
    B, T = gates.shape[:2]
    sb = sb_o.reshape(B, T, SB_WIDTH).astype(gates.dtype)
    df = (rmsnorm(df_o.reshape(B, T, DF_HEADS, DF_VDIM), g_diff) * (1.0 - lam_init))
    df = df.reshape(B, T, DF_WIDTH).astype(gates.dtype)
    ml = rmsnorm(ml_h.astype(gates.dtype), g_ml) * jax.nn.sigmoid(ml_o).reshape(B, T, ML_HEADS, ML_DH)
    ml = ml.reshape(B, T, ML_WIDTH)
    g = jax.nn.sigmoid(gates).reshape(B, T, N_BRANCH, D_MODEL)
    merged = (g[:, :, 0] * (sb @ w_br_sb) + g[:, :, 1] * (df @ w_br_df)
              + g[:, :, 2] * (ml @ w_br_ml))
    return merged @ w_out


def peer(h, w_pq, sub_keys, peer_u, peer_v):
    B, T, D = h.shape
    n_tok = B * T
    x = h.reshape(n_tok, D)
    q = (x @ w_pq).reshape(n_tok, PK_HEADS, 2, PK_HALF)
    s = jnp.einsum('thpd,hpkd->thpk', q, sub_keys).astype(jnp.float32)
    s1, i1 = lax.top_k(s[:, :, 0], PK_TOPK)
    s2, i2 = lax.top_k(s[:, :, 1], PK_TOPK)
    cand = (s1[..., :, None] + s2[..., None, :]).reshape(n_tok, PK_HEADS, PK_TOPK * PK_TOPK)
    top_s, top_i = lax.top_k(cand, PK_TOPK)
    e1 = jnp.take_along_axis(i1, top_i // PK_TOPK, axis=-1)
    e2 = jnp.take_along_axis(i2, top_i % PK_TOPK, axis=-1)
    idx = (e1 * PK_NKEYS + e2).reshape(n_tok, PK_HEADS * PK_TOPK)
    gate = jax.nn.softmax(top_s, axis=-1).reshape(n_tok, PK_HEADS * PK_TOPK)
    pad = (-n_tok) % PK_BLOCK
    n_blk = (n_tok + pad) // PK_BLOCK
    xb = jnp.pad(x, ((0, pad), (0, 0))).reshape(n_blk, PK_BLOCK, D)
    ib = jnp.pad(idx, ((0, pad), (0, 0))).reshape(n_blk, PK_BLOCK, PK_HEADS * PK_TOPK)
    gb = jnp.pad(gate, ((0, pad), (0, 0))).reshape(n_blk, PK_BLOCK, PK_HEADS * PK_TOPK)

    def experts(args):
        xt, it, gt = args
        act = jax.nn.gelu(jnp.einsum('tkd,td->tk', jnp.take(peer_u, it, axis=0), xt).astype(jnp.float32))
        return jnp.einsum('tk,tkd->td', (gt * act).astype(peer_v.dtype), jnp.take(peer_v, it, axis=0))

    y = lax.map(experts, (xb, ib, gb))
    return y.reshape(n_blk * PK_BLOCK, D)[:n_tok].reshape(B, T, D).astype(h.dtype)


def setup_inputs(seed: int = 0) -> dict:
    key = jax.random.key(seed)
    ks = iter(jax.random.split(key, 48))

    def nrm(shape, scale):
        return jax.random.normal(next(ks), shape, jnp.float32) * scale

    n_pages = PAST_LEN // PAGE_SIZE
    used = DEC_BATCH * n_pages
    n_pool = (5 * used + 3) // 4
    D = D_MODEL
    inp = {}
    inp['x_prompt'] = nrm((BATCH, SEQ, D), 1.0)
    inp['x_sample'] = nrm((DEC_BATCH, DEC_SEQ, D), 1.0)
    inp['cache_sb_k'] = nrm((DEPTH, n_pool, PAGE_SIZE, SB_HEADS, SB_DH), 1.0)
    inp['cache_sb_v'] = nrm((DEPTH, n_pool, PAGE_SIZE, SB_HEADS, SB_DH), 1.0)
    inp['cache_df_k'] = nrm((DEPTH, n_pool, PAGE_SIZE, DF_KV_HEADS, 2 * DF_DH), 1.0)
    inp['cache_df_v'] = nrm((DEPTH, n_pool, PAGE_SIZE, DF_KV_HEADS, DF_VDIM), 1.0)
    inp['state_ml_c'] = nrm((DEPTH, DEC_BATCH, ML_HEADS, ML_DH, ML_DH), 0.1)
    inp['state_ml_n'] = nrm((DEPTH, DEC_BATCH, ML_HEADS, ML_DH), 1.0)
    inp['state_ml_m'] = nrm((DEPTH, DEC_BATCH, ML_HEADS), 1.0)
    inp['state_ml_conv'] = nrm((DEPTH, DEC_BATCH, ML_CONV - 1, 2 * ML_WIDTH), 1.0)
    perm = jax.random.permutation(next(ks), n_pool)[:used]
    inp['page_table'] = perm.reshape(DEC_BATCH, n_pages).astype(jnp.int32)
    inp['c_prompt'] = nrm((BATCH, D), 1.0)
    inp['c_sample'] = nrm((DEC_BATCH, D), 1.0)
    inp['w_ada'] = nrm((DEPTH, D, 6 * D), 0.5 * D ** -0.5)
    inp['b_ada'] = nrm((DEPTH, 6 * D), 0.02)
    inp['g_norm1'] = 1.0 + nrm((DEPTH, D), 0.02)
    inp['g_norm2'] = 1.0 + nrm((DEPTH, D), 0.02)
    inp['g_final'] = 1.0 + nrm((D,), 0.02)
    inp['w_in'] = nrm((DEPTH, D, IN_WIDTH), D ** -0.5)
    inp['w_conv'] = nrm((DEPTH, ML_CONV, 2 * ML_WIDTH), 0.5)
    inp['b_conv'] = nrm((DEPTH, 2 * ML_WIDTH), 0.02)
    inp['b_ig'] = nrm((DEPTH, ML_HEADS), 0.1)
    inp['b_fg'] = 3.0 + 3.0 * jax.random.uniform(next(ks), (DEPTH, ML_HEADS), jnp.float32)
    inp['g_ml'] = 1.0 + nrm((DEPTH, ML_HEADS, ML_DH), 0.02)
    inp['lam_q1'] = nrm((DEPTH, DF_DH), 0.1)
    inp['lam_k1'] = nrm((DEPTH, DF_DH), 0.1)
    inp['lam_q2'] = nrm((DEPTH, DF_DH), 0.1)
    inp['lam_k2'] = nrm((DEPTH, DF_DH), 0.1)
    inp['g_diff'] = 1.0 + nrm((DEPTH, DF_VDIM), 0.02)
    inp['w_br_sb'] = nrm((DEPTH, SB_WIDTH, D), SB_WIDTH ** -0.5)
    inp['w_br_df'] = nrm((DEPTH, DF_WIDTH, D), DF_WIDTH ** -0.5)
    inp['w_br_ml'] = nrm((DEPTH, ML_WIDTH, D), ML_WIDTH ** -0.5)
    inp['w_out'] = nrm((DEPTH, D, D), D ** -0.5)
    inp['w_pq'] = nrm((DEPTH, D, PK_HEADS * PK_QDIM), D ** -0.5)
    inp['sub_keys'] = nrm((DEPTH, PK_HEADS, 2, PK_NKEYS, PK_HALF), PK_HALF ** -0.5)
    inp['peer_u'] = nrm((DEPTH, PK_EXPERTS, D), D ** -0.5)
    inp['peer_v'] = nrm((DEPTH, PK_EXPERTS, D), PK_HEADS ** -0.5)
    return inp


def reference(x_prompt, x_sample, cache_sb_k, cache_sb_v, cache_df_k, cache_df_v,
              state_ml_c, state_ml_n, state_ml_m, state_ml_conv, page_table,
              c_prompt, c_sample,
              w_ada, b_ada, g_norm1, g_norm2, g_final, w_in,
              w_conv, b_conv, b_ig, b_fg, g_ml,
              lam_q1, lam_k1, lam_q2, lam_k2, g_diff,
              w_br_sb, w_br_df, w_br_ml, w_out,
              w_pq, sub_keys, peer_u, peer_v):
    slopes = alibi_slopes(DF_HEADS).reshape(DF_KV_HEADS, DF_GROUP)
    seq = x_prompt.shape[1]
    dec_seq = x_sample.shape[1]
    past = page_table.shape[1] * cache_sb_k.shape[2]
    kpos_p = jnp.arange(seq, dtype=jnp.int32)
    qpos_s = past + jnp.arange(dec_seq, dtype=jnp.int32)
    kpos_s = jnp.arange(past + dec_seq, dtype=jnp.int32)
    xp, xs = x_prompt, x_sample
    new_prompt, new_sample = [], []
    for l in range(DEPTH):
        lam_init = 0.8 - 0.6 * math.exp(-0.3 * l)
        lam = (jnp.exp(jnp.sum(lam_q1[l].astype(jnp.float32) * lam_k1[l].astype(jnp.float32)))
               - jnp.exp(jnp.sum(lam_q2[l].astype(jnp.float32) * lam_k2[l].astype(jnp.float32)))
               + lam_init)
        merge_w = (g_diff[l], lam_init, g_ml[l], w_br_sb[l], w_br_df[l], w_br_ml[l], w_out[l])
        conv_w = (w_conv[l], b_conv[l], b_ig[l], b_fg[l])
        peer_w = (w_pq[l], sub_keys[l], peer_u[l], peer_v[l])

        sh1, sc1, ga1, sh2, sc2, ga2 = adaln(c_prompt, w_ada[l], b_ada[l])
        (sb_q, sb_k, sb_v, df_q, df_k, df_v, ml_qk, ml_v, ml_o, ml_i, ml_f,
         gates) = project_inputs(modulate(xp, g_norm1[l], sh1, sc1), w_in[l])
        sb_o = query_blocks(lambda qb, qp: sb_attend(qb, sb_k, sb_v, qp, kpos_p), sb_q)
        df_o = query_blocks(lambda qb, qp: diff_attend(qb, df_k, df_v, qp, kpos_p, lam, slopes), df_q)
        zero_buf = jnp.zeros((xp.shape[0], ML_CONV - 1, 2 * ML_WIDTH), xp.dtype)
        mq, mk, mv, ig, lf, conv_p = mlstm_inputs(ml_qk, ml_v, ml_i, ml_f, zero_buf, *conv_w)
        (mc_p, mn_p, mm_p), ml_h = mlstm_prompt(mq, mk, mv, ig, lf)
        xp = xp + ga1 * merge_branches(sb_o, df_o, ml_h.transpose(0, 2, 1, 3), ml_o, gates, *merge_w)
        xp = xp + ga2 * peer(modulate(xp, g_norm2[l], sh2, sc2), *peer_w)
        new_prompt.append((sb_k, sb_v, df_k, df_v, mc_p, mn_p, mm_p, conv_p))

        sh1, sc1, ga1, sh2, sc2, ga2 = adaln(c_sample, w_ada[l], b_ada[l])
        (sb_q, sb_k, sb_v, df_q, df_k, df_v, ml_qk, ml_v, ml_o, ml_i, ml_f,
         gates) = project_inputs(modulate(xs, g_norm1[l], sh1, sc1), w_in[l])
        sb_o, df_o = sample_attention(l, sb_q, sb_k, sb_v, df_q, df_k, df_v, page_table,
                                      cache_sb_k, cache_sb_v, cache_df_k, cache_df_v,
                                      lam, slopes, qpos_s, kpos_s)
        mq, mk, mv, ig, lf, conv_s = mlstm_inputs(ml_qk, ml_v, ml_i, ml_f, state_ml_conv[l], *conv_w)
        (mc_s, mn_s, mm_s), ml_h = mlstm_chunk((state_ml_c[l], state_ml_n[l], state_ml_m[l]),
                                              (mq, mk, mv, ig, lf))
        xs = xs + ga1 * merge_branches(sb_o, df_o, ml_h.transpose(0, 2, 1, 3), ml_o, gates, *merge_w)
        xs = xs + ga2 * peer(modulate(xs, g_norm2[l], sh2, sc2), *peer_w)
        new_sample.append((sb_k, sb_v, df_k, df_v, mc_s, mn_s, mm_s, conv_s))

    (sb_k_p, sb_v_p, df_k_p, df_v_p, ml_c_p, ml_n_p, ml_m_p, ml_conv_p) = [jnp.stack(z) for z in zip(*new_prompt)]
    (sb_k_s, sb_v_s, df_k_s, df_v_s, ml_c_s, ml_n_s, ml_m_s, ml_conv_s) = [jnp.stack(z) for z in zip(*new_sample)]
    y_prompt = rmsnorm(xp, g_final)
    y_sample = rmsnorm(xs, g_final)
    return (y_prompt, y_sample,
            sb_k_p, sb_v_p, df_k_p, df_v_p, ml_c_p, ml_n_p, ml_m_p, ml_conv_p,
            sb_k_s, sb_v_s, df_k_s, df_v_s, ml_c_s, ml_n_s, ml_m_s, ml_conv_s)
```

```python
import functools
import math

import jax
import jax.numpy as jnp
from jax import lax
from jax.experimental import pallas as pl
from jax.experimental.pallas import tpu as pltpu

F32 = jnp.float32
BF16 = jnp.bfloat16

D_MODEL = 1024
SB_HEADS, SB_DH = 8, 64
SB_WIDTH = SB_HEADS * SB_DH
DF_HEADS, DF_KV_HEADS, DF_DH = 8, 4, 64
DF_VDIM = 2 * DF_DH
DF_WIDTH = DF_HEADS * DF_VDIM
ML_HEADS, ML_DH = 4, 128
ML_WIDTH = ML_HEADS * ML_DH
ML_CONV = 4
ML_CHUNK = 128
PK_HEADS, PK_NKEYS, PK_QDIM, PK_TOPK = 8, 128, 256, 16
PK_HALF = PK_QDIM // 2
PK_EXPERTS = PK_NKEYS * PK_NKEYS
EPS = 1e-6
NEG = -1e30

LANES = 128
SUBLANES = 8
VMEM_LIMIT = 56 * 1024 * 1024

_OFF_ML_I = 3 * SB_WIDTH + DF_HEADS * 2 * DF_DH + DF_KV_HEADS * 2 * DF_DH + DF_KV_HEADS * DF_VDIM + 2 * ML_WIDTH + 2 * ML_WIDTH
_OFF_GATES = _OFF_ML_I + 2 * ML_HEADS
_MAIN_OUTS = (("sb_q", SB_WIDTH), ("sb_k", SB_WIDTH), ("sb_v", SB_WIDTH), ("df_q", DF_HEADS * 2 * DF_DH),
              ("df_k", DF_KV_HEADS * 2 * DF_DH), ("df_v", DF_KV_HEADS * DF_VDIM), ("ml_qk", 2 * ML_WIDTH),
              ("ml_v", ML_WIDTH), ("ml_o", ML_WIDTH), ("gates", 3 * D_MODEL))
_MAIN_WIDTH = sum(w for _, w in _MAIN_OUTS)


def _params(*sem):
    return pltpu.CompilerParams(dimension_semantics=sem, vmem_limit_bytes=VMEM_LIMIT)


def _softplus(z):
    return jnp.maximum(z, 0.0) + jnp.log1p(jnp.exp(-jnp.abs(z)))


def _sigmoid(z):
    return 1.0 / (1.0 + jnp.exp(-z))


def _rms(x):
    return x * lax.rsqrt(jnp.mean(x * x, axis=-1, keepdims=True) + EPS)


def _modulate(x, g, shift, scale):
    return (_rms(x) * g) * (1.0 + scale) + shift


def _mod_spec(mod, tm, tiles_per_row):
    return pl.BlockSpec((None, mod.shape[1], mod.shape[2]), lambda i: (i // tiles_per_row, 0, 0))


def _ada_kernel(c_ref, w_ref, b_ref, o_ref):
    c = c_ref[...]
    a = c * _sigmoid(c)
    o_ref[...] = jnp.dot(a, w_ref[...], preferred_element_type=F32, precision=lax.Precision.HIGHEST) + b_ref[...]


def adaln_all(c_all, w_ada, b_ada):
    depth, d, d6 = w_ada.shape
    r = c_all.shape[0]
    tn = 1024
    return pl.pallas_call(
        _ada_kernel,
        grid=(depth, d6 // tn),
        in_specs=[pl.BlockSpec((r, d), lambda l, j: (0, 0)),
                  pl.BlockSpec((None, d, tn), lambda l, j: (l, 0, j)),
                  pl.BlockSpec((None, 1, tn), lambda l, j: (l, 0, j))],
        out_specs=pl.BlockSpec((None, r, tn), lambda l, j: (l, 0, j)),
        out_shape=jax.ShapeDtypeStruct((depth, r, d6), F32),
        compiler_params=_params("arbitrary", "arbitrary"),
        name="adaln",
    )(c_all, w_ada, b_ada.reshape(depth, 1, d6))


def _inproj_kernel(x_ref, g_ref, sh_ref, sc_ref, wm_ref, wif_ref, *out_refs):
    h = _modulate(x_ref[...], g_ref[...], sh_ref[...], sc_ref[...])
    hb = h.astype(BF16)
    off = 0
    for (_, width), o_ref in zip(_MAIN_OUTS, out_refs[:-1]):
        o_ref[...] = jnp.dot(hb, wm_ref[:, off:off + width], preferred_element_type=F32)
        off += width
    out_refs[-1][...] = jnp.dot(h, wif_ref[...], preferred_element_type=F32, precision=lax.Precision.HIGHEST)


def prep_inproj_weights(w_in_l):
    w_main = jnp.concatenate([w_in_l[:, :_OFF_ML_I], w_in_l[:, _OFF_GATES:]], axis=1).astype(BF16)
    w_if = jnp.pad(w_in_l[:, _OFF_ML_I:_OFF_GATES], ((0, 0), (0, LANES - 2 * ML_HEADS)))
    return w_main, w_if


def inproj(x, g, shift, scale, w_main, w_if, *, tm, tiles_per_row):
    n, d = x.shape
    row = lambda width: pl.BlockSpec((tm, width), lambda i: (i, 0))
    const = lambda shape: pl.BlockSpec(shape, lambda i: (0,) * len(shape), pipeline_mode=pl.Buffered(1))
    outs = pl.pallas_call(
        _inproj_kernel,
        grid=(n // tm,),
        in_specs=[row(d), const((1, d)), _mod_spec(shift, tm, tiles_per_row), _mod_spec(scale, tm, tiles_per_row),
                  const(w_main.shape), const(w_if.shape)],
        out_specs=[row(w) for _, w in _MAIN_OUTS] + [row(LANES)],
        out_shape=[jax.ShapeDtypeStruct((n, w), F32) for _, w in _MAIN_OUTS] + [jax.ShapeDtypeStruct((n, LANES), F32)],
        compiler_params=_params("arbitrary"),
        name="inproj",
    )(x, g.reshape(1, d), shift, scale, w_main, w_if)
    res = {name: o for (name, _), o in zip(_MAIN_OUTS, outs[:-1])}
    res["ml_if"] = outs[-1]
    return res


def _merge_kernel(x_ref, ga_ref, sb_ref, df_ref, mlh_ref, mlo_ref, gt_ref, gdf_ref, gml_ref,
                  wsb_ref, wdf_ref, wml_ref, wout_ref, o_ref, *, lam_init):
    d = x_ref.shape[-1]
    sb = jnp.dot(sb_ref[...].astype(BF16), wsb_ref[...], preferred_element_type=F32)
    gdf = gdf_ref[...] * (1.0 - lam_init)
    dfn = [(_rms(df_ref[:, h * DF_VDIM:(h + 1) * DF_VDIM]) * gdf).astype(BF16) for h in range(DF_HEADS)]
    df = jnp.dot(jnp.concatenate(dfn, axis=1), wdf_ref[...], preferred_element_type=F32)
    mln = []
    for h in range(ML_HEADS):
        s = slice(h * ML_DH, (h + 1) * ML_DH)
        mln.append((_rms(mlh_ref[:, s]) * gml_ref[:, s] * _sigmoid(mlo_ref[:, s])).astype(BF16))
    ml = jnp.dot(jnp.concatenate(mln, axis=1), wml_ref[...], preferred_element_type=F32)
    merged = (_sigmoid(gt_ref[:, 0:d]) * sb + _sigmoid(gt_ref[:, d:2 * d]) * df
              + _sigmoid(gt_ref[:, 2 * d:3 * d]) * ml)
    y = jnp.dot(merged.astype(BF16), wout_ref[...], preferred_element_type=F32)
    o_ref[...] = x_ref[...] + ga_ref[...] * y


def merge(x, ga, sb_o, df_o, ml_h, ml_o, gates, g_diff, g_ml, wsb, wdf, wml, wout, *, lam_init, tm, tiles_per_row):
    n, d = x.shape
    row = lambda width: pl.BlockSpec((tm, width), lambda i: (i, 0))
    const = lambda shape: pl.BlockSpec(shape, lambda i: (0,) * len(shape))
    return pl.pallas_call(
        functools.partial(_merge_kernel, lam_init=lam_init),
        grid=(n // tm,),
        in_specs=[row(d), _mod_spec(ga, tm, tiles_per_row), row(SB_WIDTH), row(DF_WIDTH), row(ML_WIDTH), row(ML_WIDTH),
                  row(3 * d), const((1, DF_VDIM)), const((1, ML_WIDTH)),
                  const(wsb.shape), const(wdf.shape), const(wml.shape), const(wout.shape)],
        out_specs=row(d),
        out_shape=jax.ShapeDtypeStruct((n, d), F32),
        compiler_params=_params("arbitrary"),
        name="merge",
    )(x, ga, sb_o, df_o, ml_h, ml_o, gates, g_diff.reshape(1, DF_VDIM), g_ml.reshape(1, ML_WIDTH), wsb, wdf, wml, wout)


def _final_norm_kernel(x_ref, g_ref, o_ref):
    o_ref[...] = _rms(x_ref[...]) * g_ref[...]


def final_norm(x, g, *, tm):
    n, d = x.shape
    return pl.pallas_call(
        _final_norm_kernel,
        grid=(n // tm,),
        in_specs=[pl.BlockSpec((tm, d), lambda i: (i, 0)), pl.BlockSpec((1, d), lambda i: (0, 0))],
        out_specs=pl.BlockSpec((tm, d), lambda i: (i, 0)),
        out_shape=jax.ShapeDtypeStruct((n, d), F32),
        compiler_params=_params("arbitrary"),
        name="final_norm",
    )(x, g.reshape(1, d))


def _split_bf16(x):
    hi = x.astype(BF16)
    return hi, (x - hi.astype(F32)).astype(BF16)


def _sb_kernel(q_ref, k_ref, v_ref, o_ref, *, tq):
    tk = tq
    qi = pl.program_id(2)
    lane = lax.broadcasted_iota(jnp.int32, (tq, LANES), 1)
    q = q_ref[...] * (SB_DH ** -0.5)
    row = lax.broadcasted_iota(jnp.int32, (tq, tk), 0)
    col = lax.broadcasted_iota(jnp.int32, (tq, tk), 1)
    earlier = col < row
    suffix = (row > col).astype(BF16)
    ones = jnp.ones((tk, LANES), BF16)
    nt = (((1,), (1,)), ((), ()))
    heads = []
    for h in range(LANES // SB_DH):
        qh = jnp.where(lane // SB_DH == h, q, 0.0).astype(BF16)

        def tile(j, carry, masked, qh=qh):
            acc, run = carry
            start = pl.multiple_of(j * tk, tk)
            kt = k_ref[pl.ds(start, tk), :].astype(BF16)
            vt = v_ref[pl.ds(start, tk), :].astype(BF16)
            z = lax.dot_general(qh, kt, nt, preferred_element_type=F32)
            sp = _softplus(z)
            log_keep = jnp.where(earlier, -sp, 0.0) if masked else -sp
            hi, lo = _split_bf16(log_keep)
            within = jnp.dot(hi, suffix, preferred_element_type=F32) + jnp.dot(lo, suffix, preferred_element_type=F32)
            total = jnp.dot(hi, ones, preferred_element_type=F32) + jnp.dot(lo, ones, preferred_element_type=F32)
            logw = (z - sp) + within + jnp.concatenate([run] * (tk // LANES), axis=1)
            w = jnp.exp(logw)
            if masked:
                w = jnp.where(earlier, w, 0.0)
            acc = acc + jnp.dot(w.astype(BF16), vt, preferred_element_type=F32)
            return acc, run + total

        zero = jnp.zeros((tq, LANES), F32)
        carry = tile(qi, (zero, zero), True)
        carry = lax.fori_loop(0, qi, lambda i, c, tile=tile: tile(qi - 1 - i, c, False), carry)
        heads.append(carry[0])
    o_ref[...] = jnp.where(lane < SB_DH, heads[0], heads[1])


def sb_attention(q, k, v, *, batch, tq):
    n, width = q.shape
    t = n // batch
    nq = t // tq
    kv = lambda a: a.reshape(batch, t, width)
    kv_spec = pl.BlockSpec((None, t, LANES), lambda b, hp, i: (b, 0, hp))
    q_spec = pl.BlockSpec((tq, LANES), lambda b, hp, i: (b * nq + i, hp))
    return pl.pallas_call(
        functools.partial(_sb_kernel, tq=tq),
        grid=(batch, width // LANES, nq),
        in_specs=[q_spec, kv_spec, kv_spec],
        out_specs=q_spec,
        out_shape=jax.ShapeDtypeStruct((n, width), F32),
        compiler_params=_params("arbitrary", "arbitrary", "arbitrary"),
        name="sb_attention",
    )(q, kv(k), kv(v))


def _lambda_value(lam_ref, lam_init):
    lv = lam_ref[...]
    s1 = jnp.sum(lv[0:1, :] * lv[1:2, :], axis=-1, keepdims=True)
    s2 = jnp.sum(lv[2:3, :] * lv[3:4, :], axis=-1, keepdims=True)
    return jnp.exp(s1) - jnp.exp(s2) + lam_init


def _df_kernel(slope_ref, q_ref, k_ref, v_ref, lam_ref, o_ref, *, tq, lam_init):
    tk = tq
    hd = pl.program_id(1)
    qi = pl.program_id(2)
    slope = slope_ref[hd]
    lane = lax.broadcasted_iota(jnp.int32, (tq, LANES), 1)
    q = q_ref[...] * (DF_DH ** -0.5)
    row = lax.broadcasted_iota(jnp.int32, (tq, tk), 0)
    col = lax.broadcasted_iota(jnp.int32, (tq, tk), 1)
    visible = col <= row
    rel = (col - row).astype(F32) * slope
    ones = jnp.ones((tk, LANES), BF16)
    nt = (((1,), (1,)), ((), ()))
    results = []
    for m in range(2):
        qm = jnp.where(lane // DF_DH == m, q, 0.0).astype(BF16)

        def tile(j, carry, masked, qm=qm):
            acc, mx = carry
            start = pl.multiple_of(j * tk, tk)
            kt = k_ref[pl.ds(start, tk), :].astype(BF16)
            vt = jnp.concatenate([v_ref[pl.ds(start, tk), :].astype(BF16), ones], axis=1)
            s = lax.dot_general(qm, kt, nt, preferred_element_type=F32) + rel
            if masked:
                s = jnp.where(visible, s, NEG)
            off = -slope * ((qi - j) * tk).astype(F32)
            mx_new = jnp.maximum(mx, jnp.max(s, axis=-1, keepdims=True) + off)
            e = jnp.exp(s + (off - mx_new))
            acc = jnp.exp(mx - mx_new) * acc + jnp.dot(e.astype(BF16), vt, preferred_element_type=F32)
            return acc, mx_new

        carry = (jnp.zeros((tq, 2 * LANES), F32), jnp.full((tq, 1), NEG, F32))
        carry = lax.fori_loop(0, qi, lambda j, c, tile=tile: tile(j, c, False), carry)
        acc, _ = tile(qi, carry, True)
        results.append(acc[:, :LANES] / acc[:, LANES:])
    o_ref[...] = results[0] - _lambda_value(lam_ref, lam_init) * results[1]


def alibi_slope_table():
    return jnp.array([2.0 ** (-8.0 * (h + 1) / DF_HEADS) for h in range(DF_HEADS)], dtype=F32)


def df_attention(q, k, v, lam_vecs, *, lam_init, batch, tq):
    n, width = q.shape
    t = n // batch
    nq = t // tq
    kv = lambda a: a.reshape(batch, t, a.shape[-1])
    group = DF_HEADS // DF_KV_HEADS
    kv_spec = pl.BlockSpec((None, t, LANES), lambda b, h, i, s: (b, 0, h // group))
    q_spec = pl.BlockSpec((tq, LANES), lambda b, h, i, s: (b * nq + i, h))
    return pl.pallas_call(
        functools.partial(_df_kernel, tq=tq, lam_init=lam_init),
        grid_spec=pltpu.PrefetchScalarGridSpec(
            num_scalar_prefetch=1,
            grid=(batch, DF_HEADS, nq),
            in_specs=[q_spec, kv_spec, kv_spec, pl.BlockSpec(lam_vecs.shape, lambda b, h, i, s: (0, 0))],
            out_specs=q_spec),
        out_shape=jax.ShapeDtypeStruct((n, width), F32),
        compiler_params=_params("arbitrary", "arbitrary", "arbitrary"),
        name="df_attention",
    )(alibi_slope_table(), q, kv(k), kv(v), lam_vecs)


def _mlstm_kernel(qk_ref, v_ref, if_ref, conv0_ref, c0_ref, n0_ref, m0_ref, wc_ref, bc_ref, bif_ref,
                  h_ref, c_ref, n_ref, m_ref, buf, cs, ns, ms, *, valid):
    L = ML_CHUNK
    ci = pl.program_id(1)

    @pl.when(ci == 0)
    def _():
        buf[0:SUBLANES, :] = conv0_ref[...]
        cs[...] = c0_ref[...]
        ns[...] = n0_ref[...]
        ms[...] = m0_ref[...]

    cur = qk_ref[...]
    buf[SUBLANES:SUBLANES + L, :] = cur
    acc = bc_ref[...]
    for j in range(ML_CONV):
        lo = SUBLANES - (ML_CONV - 1) + j
        acc = acc + buf[lo:lo + L, :] * wc_ref[j:j + 1, :]
    qk = acc * _sigmoid(acc)
    buf[0:SUBLANES, :] = cur[L - SUBLANES:L, :]

    row = lax.broadcasted_iota(jnp.int32, (L, LANES), 0)
    lane = lax.broadcasted_iota(jnp.int32, (L, LANES), 1)
    gt = if_ref[...] + bif_ref[...]
    is_valid = row < valid
    ig = jnp.where(is_valid, gt, NEG)
    lf = jnp.where(is_valid & (lane >= ML_HEADS) & (lane < 2 * ML_HEADS), -_softplus(-gt), 0.0)
    r2 = lax.broadcasted_iota(jnp.int32, (L, L), 0)
    c2 = lax.broadcasted_iota(jnp.int32, (L, L), 1)
    causal = c2 <= r2
    bcum = jnp.dot(causal.astype(F32), lf, preferred_element_type=F32, precision=lax.Precision.HIGHEST)
    ig_t = ig.T
    bcum_t = bcum.T
    nt = (((1,), (1,)), ((), ()))
    for h in range(ML_HEADS):
        b_col = bcum[:, ML_HEADS + h:ML_HEADS + h + 1]
        b_row = bcum_t[ML_HEADS + h:ML_HEADS + h + 1, :]
        ig_row = ig_t[h:h + 1, :]
        ig_col = ig[:, h:h + 1]
        m_run = ms[h:h + 1, 0:1]
        log_d = jnp.where(causal, b_col - b_row + ig_row, NEG)
        log_inter = b_col + m_run
        m_t = jnp.maximum(log_inter, jnp.max(log_d, axis=-1, keepdims=True))
        a_inter = jnp.exp(log_inter - m_t)
        dmat = jnp.exp(log_d - m_t)
        hs = slice(h * ML_DH, (h + 1) * ML_DH)
        qh = qk[:, hs]
        kh = qk[:, ML_WIDTH + h * ML_DH:ML_WIDTH + (h + 1) * ML_DH] * (ML_DH ** -0.5)
        qb, kb, vb = qh.astype(BF16), kh.astype(BF16), v_ref[:, hs].astype(BF16)
        scores = lax.dot_general(qb, kb, nt, preferred_element_type=F32) * dmat
        c_mat = cs[h]
        n_vec = ns[h:h + 1, :]
        num = (a_inter * jnp.dot(qb, c_mat.astype(BF16), preferred_element_type=F32)
               + jnp.dot(scores.astype(BF16), vb, preferred_element_type=F32))
        den = a_inter * jnp.sum(qh * n_vec, axis=-1, keepdims=True) + jnp.sum(scores, axis=-1, keepdims=True)
        h_ref[:, hs] = num / jnp.maximum(jnp.abs(den), jnp.exp(-m_t))
        m_new = m_t[L - 1:L, :]
        b_last = b_col[L - 1:L, :]
        w_end = jnp.exp(b_last - b_col + ig_col - m_new)
        decay = jnp.exp(b_last + m_run - m_new)
        kw = kh * w_end
        cs[h] = decay * c_mat + jnp.dot(kw.T.astype(BF16), vb, preferred_element_type=F32)
        ns[h:h + 1, :] = decay * n_vec + jnp.sum(kw, axis=0, keepdims=True)
        ms[h:h + 1, :] = jnp.broadcast_to(m_new, (1, LANES))

    @pl.when(ci == pl.num_programs(1) - 1)
    def _():
        c_ref[...] = cs[...]
        n_ref[...] = ns[...]
        m_ref[...] = ms[...]


def mlstm(ml_qk, ml_v, ml_if, conv8, c0, n0, m0, w_conv, b_conv, b_ig, b_fg, *, batch, valid):
    n_tok = ml_qk.shape[0]
    L = ML_CHUNK
    nc = n_tok // batch // L
    pad_rows = lambda a: jnp.pad(a, ((0, 0), (0, SUBLANES - ML_HEADS), (0, 0)))
    n0p = pad_rows(n0)
    m0p = pad_rows(jnp.broadcast_to(m0[:, :, None], (batch, ML_HEADS, LANES)))
    bias_if = jnp.pad(jnp.concatenate([b_ig, b_fg]), (0, LANES - 2 * ML_HEADS)).reshape(1, LANES)
    tok = lambda width: pl.BlockSpec((L, width), lambda b, c: (b * nc + c, 0))
    per_b = lambda *shape: pl.BlockSpec((None,) + shape, lambda b, c: (b,) + (0,) * len(shape))
    const = lambda *shape: pl.BlockSpec(shape, lambda b, c: (0,) * len(shape))
    h, c_out, n_out, m_out = pl.pallas_call(
        functools.partial(_mlstm_kernel, valid=valid),
        grid=(batch, nc),
        in_specs=[tok(2 * ML_WIDTH), tok(ML_WIDTH), tok(LANES), per_b(SUBLANES, 2 * ML_WIDTH),
                  per_b(ML_HEADS, ML_DH, ML_DH), per_b(SUBLANES, LANES), per_b(SUBLANES, LANES),
                  const(ML_CONV, 2 * ML_WIDTH), const(1, 2 * ML_WIDTH), const(1, LANES)],
        out_specs=[tok(ML_WIDTH), per_b(ML_HEADS, ML_DH, ML_DH), per_b(SUBLANES, LANES), per_b(SUBLANES, LANES)],
        out_shape=[jax.ShapeDtypeStruct((n_tok, ML_WIDTH), F32), jax.ShapeDtypeStruct((batch, ML_HEADS, ML_DH, ML_DH), F32),
                   jax.ShapeDtypeStruct((batch, SUBLANES, LANES), F32), jax.ShapeDtypeStruct((batch, SUBLANES, LANES), F32)],
        scratch_shapes=[pltpu.VMEM((SUBLANES + L, 2 * ML_WIDTH), F32), pltpu.VMEM((ML_HEADS, ML_DH, ML_DH), F32),
                        pltpu.VMEM((SUBLANES, LANES), F32), pltpu.VMEM((SUBLANES, LANES), F32)],
        compiler_params=_params("arbitrary", "arbitrary"),
        name="mlstm",
    )(ml_qk, ml_v, ml_if, conv8, c0, n0p, m0p, w_conv, b_conv.reshape(1, -1), bias_if)
    return h, c_out, n_out[:, :ML_HEADS], m_out[:, :ML_HEADS, 0]


def _top16(s):
    rows = lax.broadcasted_iota(jnp.int32, (PK_TOPK, s.shape[1]), 0)
    work = s
    rank = jnp.full(s.shape, float(PK_TOPK), F32)
    top = jnp.zeros((PK_TOPK, s.shape[1]), F32)
    for k in range(PK_TOPK):
        m = jnp.max(work, axis=0, keepdims=True)
        sel = work == m
        rank = jnp.where(sel, float(k), rank)
        work = jnp.where(sel, NEG, work)
        top = jnp.where(rows == k, m, top)
    return top, rank


def _route_kernel(x_ref, g_ref, sh_ref, sc_ref, wpq_ref, keys_ref, xt_ref, r2_ref, nb_ref, e1_ref, e2_ref):
    tm = x_ref.shape[0]
    x2 = _modulate(x_ref[...], g_ref[...], sh_ref[...], sc_ref[...])
    xt_ref[...] = x2.T.astype(BF16)
    q = jnp.dot(x2.astype(BF16), wpq_ref[...], preferred_element_type=F32)
    nt = (((1,), (1,)), ((), ()))
    row8 = lax.broadcasted_iota(jnp.int32, (SUBLANES, tm), 0)
    for h in range(PK_HEADS):
        s, top, rank = [], [], []
        for p in range(2):
            lo = (2 * h + p) * PK_HALF
            sp = lax.dot_general(keys_ref[h, p], q[:, lo:lo + PK_HALF].astype(BF16), nt, preferred_element_type=F32)
            tp, rk = _top16(sp)
            s.append(sp)
            top.append(tp)
            rank.append(rk)
        blocks = [top[0][0:1, :] + top[1]]
        for a in range(1, PK_TOPK):
            width = PK_TOPK // (a + 1)
            blocks.append(jnp.where(row8 < width, top[0][a:a + 1, :] + top[1][0:SUBLANES, :], NEG))
        cand = jnp.concatenate(blocks, axis=0)
        work = cand
        for k in range(PK_TOPK):
            thr = jnp.max(work, axis=0, keepdims=True)
            work = jnp.where(work == thr, NEG, work)
        chosen = cand >= thr
        cmax = top[0][0:1, :] + top[1][0:1, :]
        z = jnp.sum(jnp.where(chosen, jnp.exp(cand - cmax), 0.0), axis=0, keepdims=True)
        chosen_f = chosen.astype(F32)
        nb_rows = jnp.zeros_like(rank[0])
        lo = 0
        for a in range(PK_TOPK):
            size = PK_TOPK if a == 0 else SUBLANES
            nb_a = jnp.sum(chosen_f[lo:lo + size, :], axis=0, keepdims=True)
            nb_rows = jnp.where(rank[0] == float(a), nb_a, nb_rows)
            lo += size
        r2_ref[h] = rank[1]
        nb_ref[h] = nb_rows
        e1_ref[h] = jnp.exp(s[0] - top[0][0:1, :]) / z
        e2_ref[h] = jnp.exp(s[1] - top[1][0:1, :])


def peer_route(x, g, shift, scale, w_pq, sub_keys, *, tm, tiles_per_row):
    n, d = x.shape
    const = lambda shape: pl.BlockSpec(shape, lambda i: (0,) * len(shape))
    lanes = pl.BlockSpec((PK_HEADS, PK_NKEYS, tm), lambda i: (0, 0, i))
    routed = jax.ShapeDtypeStruct((PK_HEADS, PK_NKEYS, n), F32)
    return pl.pallas_call(
        _route_kernel,
        grid=(n // tm,),
        in_specs=[pl.BlockSpec((tm, d), lambda i: (i, 0)), const((1, d)), _mod_spec(shift, tm, tiles_per_row),
                  _mod_spec(scale, tm, tiles_per_row), const(w_pq.shape), const(sub_keys.shape)],
        out_specs=[pl.BlockSpec((d, tm), lambda i: (0, i)), lanes, lanes, lanes, lanes],
        out_shape=[jax.ShapeDtypeStruct((d, n), BF16), routed, routed, routed, routed],
        compiler_params=_params("arbitrary"),
        name="peer_route",
    )(x, g.reshape(1, d), shift, scale, w_pq, sub_keys)


def _gelu_tanh(x):
    return 0.5 * x * (1.0 + jnp.tanh(math.sqrt(2.0 / math.pi) * (x + 0.044715 * (x * x * x))))


def _expert_kernel(x_ref, ga_ref, xt_ref, r2_ref, nb_ref, e1_ref, e2_ref, u_ref, vt_ref, o_ref, acc):
    c = pl.program_id(1)

    @pl.when(c == 0)
    def _():
        acc[...] = jnp.zeros_like(acc)

    act = _gelu_tanh(jnp.dot(u_ref[...], xt_ref[...], preferred_element_type=F32))
    slabs = []
    for e in range(u_ref.shape[0] // PK_NKEYS):
        gate = None
        for h in range(PK_HEADS):
            term = jnp.where(r2_ref[h] < nb_ref[h, e:e + 1, :], e2_ref[h], 0.0) * e1_ref[h, e:e + 1, :]
            gate = term if gate is None else gate + term
        slabs.append((gate * act[e * PK_NKEYS:(e + 1) * PK_NKEYS, :]).astype(BF16))
    acc[...] += jnp.dot(vt_ref[...], jnp.concatenate(slabs, axis=0), preferred_element_type=F32)

    @pl.when(c == pl.num_programs(1) - 1)
    def _():
        o_ref[...] = x_ref[...] + ga_ref[...] * acc[...].T


def peer_expert(x, ga, xt, r2, nb, e1c, e2, u, vt, *, tm, tiles_per_row, chunk=2048):
    n, d = x.shape
    n_exp = u.shape[0]
    e1_per_chunk = chunk // PK_NKEYS
    lanes = pl.BlockSpec((PK_HEADS, PK_NKEYS, tm), lambda i, c: (0, 0, i))
    by_e1 = pl.BlockSpec((PK_HEADS, e1_per_chunk, tm), lambda i, c: (0, c, i))
    mod = pl.BlockSpec((None, ga.shape[1], ga.shape[2]), lambda i, c: (i // tiles_per_row, 0, 0))
    return pl.pallas_call(
        _expert_kernel,
        grid=(n // tm, n_exp // chunk),
        in_specs=[pl.BlockSpec((tm, d), lambda i, c: (i, 0)), mod, pl.BlockSpec((d, tm), lambda i, c: (0, i)),
                  lanes, by_e1, by_e1, lanes,
                  pl.BlockSpec((chunk, d), lambda i, c: (c, 0)), pl.BlockSpec((d, chunk), lambda i, c: (0, c))],
        out_specs=pl.BlockSpec((tm, d), lambda i, c: (i, 0)),
        out_shape=jax.ShapeDtypeStruct((n, d), F32),
        scratch_shapes=[pltpu.VMEM((d, tm), F32)],
        compiler_params=_params("arbitrary", "arbitrary"),
        name="peer_expert",
    )(x, ga, xt, r2, nb, e1c, e2, u, vt)


PAGES_PER_STEP = 4


def _page_specs(n_pages, rows, width, layer, pps):
    return [pl.BlockSpec((None, None, rows, width),
                         lambda b, s, pt, k=k: (layer, pt[b, n_pages - 1 - (s * pps + k)], 0, 0)) for k in range(pps)]


def _sbs_kernel(pt_ref, qb_ref, kn_ref, vn_ref, *refs, pps):
    k_refs, v_refs, o_ref, acc, run = refs[:pps], refs[pps:2 * pps], refs[2 * pps], refs[2 * pps + 1], refs[2 * pps + 2]
    s = pl.program_id(1)
    n_q, page = run.shape
    t_new = o_ref.shape[0]
    qb = (qb_ref[...] * (SB_DH ** -0.5)).astype(BF16)
    r = lax.broadcasted_iota(jnp.int32, (page, page), 0)
    c = lax.broadcasted_iota(jnp.int32, (page, page), 1)
    suffix = (r > c).astype(BF16)
    nt = (((1,), (1,)), ((), ()))

    def process(k_ref, v_ref, mask):
        z = jnp.dot(qb, k_ref[...].astype(BF16), preferred_element_type=F32)
        sp = _softplus(z)
        log_keep = -sp if mask is None else jnp.where(mask, -sp, 0.0)
        hi, lo = _split_bf16(log_keep)
        within = jnp.dot(hi, suffix, preferred_element_type=F32) + jnp.dot(lo, suffix, preferred_element_type=F32)
        w = jnp.exp((z - sp) + within + run[...])
        if mask is not None:
            w = jnp.where(mask, w, 0.0)
        acc[...] += lax.dot_general(w.astype(BF16), v_ref[...].astype(BF16), nt, preferred_element_type=F32)
        run[...] += jnp.sum(log_keep, axis=1, keepdims=True)

    @pl.when(s == 0)
    def _():
        acc[...] = jnp.zeros_like(acc)
        run[...] = jnp.zeros_like(run)
        pos = lax.broadcasted_iota(jnp.int32, (n_q, page), 1)
        t_row = lax.broadcasted_iota(jnp.int32, (n_q, page), 0) // SB_HEADS
        process(kn_ref, vn_ref, pos < t_row)

    for k in range(pps):
        process(k_refs[k], v_refs[k], None)

    @pl.when(s == pl.num_programs(1) - 1)
    def _():
        h_row = lax.broadcasted_iota(jnp.int32, (SB_HEADS, SB_WIDTH), 0)
        h_col = lax.broadcasted_iota(jnp.int32, (SB_HEADS, SB_WIDTH), 1) // SB_DH
        own = acc[...].reshape(t_new, SB_HEADS, SB_WIDTH) * (h_row == h_col).astype(F32)
        o_ref[...] = jnp.sum(own, axis=1)


def sb_attention_sample(page_table, q, k_new, v_new, cache_k, cache_v, *, layer, pps=PAGES_PER_STEP):
    nb, t_new, width = q.shape
    n_pages = page_table.shape[1]
    depth, n_pool, page, heads, dh = cache_k.shape
    n_q = t_new * heads
    eye = jnp.eye(heads, dtype=F32)[None, None, :, :, None]
    qb = (q.reshape(nb, t_new, heads, 1, dh) * eye).reshape(nb, n_q, width)
    new_page = lambda a: jnp.pad(a.transpose(0, 2, 1), ((0, 0), (0, 0), (0, page - t_new)))
    paged = lambda a: a.transpose(0, 1, 3, 4, 2).reshape(depth, n_pool, width, page)
    per_b = lambda *shape: pl.BlockSpec((None,) + shape, lambda b, s, pt: (b,) + (0,) * len(shape))
    return pl.pallas_call(
        functools.partial(_sbs_kernel, pps=pps),
        grid_spec=pltpu.PrefetchScalarGridSpec(
            num_scalar_prefetch=1,
            grid=(nb, n_pages // pps),
            in_specs=[per_b(n_q, width), per_b(width, page), per_b(width, page)]
            + _page_specs(n_pages, width, page, layer, pps) + _page_specs(n_pages, width, page, layer, pps),
            out_specs=per_b(t_new, width),
            scratch_shapes=[pltpu.VMEM((n_q, width), F32), pltpu.VMEM((n_q, page), F32)]),
        out_shape=jax.ShapeDtypeStruct((nb, t_new, width), F32),
        compiler_params=_params("arbitrary", "arbitrary"),
        name="sb_attention_sample",
    )(page_table, qb, new_page(k_new), new_page(v_new), *([paged(cache_k)] * pps), *([paged(cache_v)] * pps))


def _dfs_kernel(pt_ref, qc_ref, kn_ref, vn_ref, col_ref, lam_ref, *refs, pps, n_pages, lam_init):
    k_refs, v_refs, o_ref, acc, mx = refs[:pps], refs[pps:2 * pps], refs[2 * pps], refs[2 * pps + 1], refs[2 * pps + 2]
    s = pl.program_id(1)
    rows = kn_ref.shape[0]
    page = rows // DF_KV_HEADS
    half = LANES // 4
    r = lax.broadcasted_iota(jnp.int32, (rows, LANES), 0)
    c = lax.broadcasted_iota(jnp.int32, (rows, LANES), 1)
    group = DF_HEADS // DF_KV_HEADS
    same_kv = (r % DF_KV_HEADS == (c // group) % DF_KV_HEADS) & (c < 2 * half)
    slope = col_ref[0:1, :]
    t_col = col_ref[1:2, :]
    tok = (r // DF_KV_HEADS).astype(F32)
    slope_tok = slope * tok
    qc = (qc_ref[...] * (DF_DH ** -0.5)).astype(BF16)
    ones = jnp.ones((LANES, rows), BF16)

    def process(k_ref, v_ref, mask, base):
        z = jnp.dot(k_ref[...].astype(BF16), qc, preferred_element_type=F32)
        sb = jnp.where(mask, z + slope_tok + slope * (base - t_col), NEG)
        mx_old = mx[0:1, :]
        mx_new = jnp.maximum(mx_old, jnp.max(sb, axis=0, keepdims=True))
        e = jnp.exp(sb - mx_new).astype(BF16)
        vext = jnp.concatenate([v_ref[...].T.astype(BF16), ones], axis=0)
        acc[...] = jnp.exp(mx_old - mx_new) * acc[...] + jnp.dot(vext, e, preferred_element_type=F32)
        mx[...] = jnp.broadcast_to(mx_new, mx.shape)

    @pl.when(s == 0)
    def _():
        acc[...] = jnp.zeros_like(acc)
        mx[...] = jnp.full_like(mx, NEG)
        process(kn_ref, vn_ref, same_kv & (tok <= t_col), 0.0)

    for k in range(pps):
        page_index = n_pages - 1 - (s * pps + k)
        process(k_refs[k], v_refs[k], same_kv, ((page_index - n_pages) * page).astype(F32))

    @pl.when(s == pl.num_programs(1) - 1)
    def _():
        res = acc[0:LANES, :] / acc[LANES:2 * LANES, :]
        o_ref[...] = res - _lambda_value(lam_ref, lam_init) * pltpu.roll(res, LANES - half, axis=1)


def df_attention_sample(page_table, q, k_new, v_new, cache_k, cache_v, lam_vecs, *, layer, lam_init, pps=PAGES_PER_STEP):
    nb, t_new, _ = q.shape
    n_pages = page_table.shape[1]
    depth, n_pool, page, kvh, width = cache_k.shape
    rows = page * kvh
    group = DF_HEADS // DF_KV_HEADS
    half = LANES // 4
    assert t_new * kvh * group == half and width == LANES
    qt = q.reshape(nb, t_new * kvh * group, width).transpose(0, 2, 1)
    d_half = (jnp.arange(width) // DF_DH)[None, :, None]
    qc = jnp.concatenate([jnp.where(d_half == 0, qt, 0.0), jnp.where(d_half == 1, qt, 0.0),
                          jnp.zeros((nb, width, LANES - 2 * half), F32)], axis=2)
    col = jnp.arange(LANES)
    head_of_col = ((col // group) % kvh) * group + col % group
    slope_col = jnp.where(col < 2 * half, alibi_slope_table()[head_of_col], 0.0)
    t_of_col = ((col % half) // (kvh * group)).astype(F32)
    col_info = jnp.zeros((SUBLANES, LANES), F32).at[0].set(slope_col).at[1].set(t_of_col)
    pad_page = lambda a: jnp.pad(a, ((0, 0), (0, rows - a.shape[1]), (0, 0)))
    paged = lambda a: a.reshape(depth, n_pool, rows, width)
    per_b = lambda *shape: pl.BlockSpec((None,) + shape, lambda b, s, pt: (b,) + (0,) * len(shape))
    const = lambda *shape: pl.BlockSpec(shape, lambda b, s, pt: (0,) * len(shape))
    out = pl.pallas_call(
        functools.partial(_dfs_kernel, pps=pps, n_pages=n_pages, lam_init=lam_init),
        grid_spec=pltpu.PrefetchScalarGridSpec(
            num_scalar_prefetch=1,
            grid=(nb, n_pages // pps),
            in_specs=[per_b(width, LANES), per_b(rows, width), per_b(rows, width), const(SUBLANES, LANES), const(*lam_vecs.shape)]
            + _page_specs(n_pages, rows, width, layer, pps) + _page_specs(n_pages, rows, width, layer, pps),
            out_specs=per_b(width, LANES),
            scratch_shapes=[pltpu.VMEM((2 * LANES, LANES), F32), pltpu.VMEM((SUBLANES, LANES), F32)]),
        out_shape=jax.ShapeDtypeStruct((nb, width, LANES), F32),
        compiler_params=_params("arbitrary", "arbitrary"),
        name="df_attention_sample",
    )(page_table, qc, pad_page(k_new), pad_page(v_new), col_info, lam_vecs, *([paged(cache_k)] * pps), *([paged(cache_v)] * pps))
    return out[:, :, :half].transpose(0, 2, 1).reshape(nb, t_new, kvh * group * width)


TOKEN_TILE = 256
ATTN_TILE = 256


def _token_stage(x, mods, attend, lam_init, weights, *, tm, tiles_per_row):
    (g1, g2, g_diff_l, g_ml_l, w_main, w_if, wsb, wdf, wml, wout, wpq, keys, u, vt) = weights
    p = inproj(x, g1, mods[0], mods[1], w_main, w_if, tm=tm, tiles_per_row=tiles_per_row)
    sb_o, df_o, ml_h, state = attend(p)
    x = merge(x, mods[2], sb_o, df_o, ml_h, p["ml_o"], p["gates"], g_diff_l, g_ml_l, wsb, wdf, wml, wout,
              lam_init=lam_init, tm=tm, tiles_per_row=tiles_per_row)
    routed = peer_route(x, g2, mods[3], mods[4], wpq, keys, tm=tm, tiles_per_row=tiles_per_row)
    x = peer_expert(x, mods[5], *routed, u, vt, tm=tm, tiles_per_row=tiles_per_row)
    return x, p, state


def kernel(x_prompt, x_sample, cache_sb_k, cache_sb_v, cache_df_k, cache_df_v, state_ml_c, state_ml_n, state_ml_m, state_ml_conv, page_table, c_prompt, c_sample, w_ada, b_ada, g_norm1, g_norm2, g_final, w_in, w_conv, b_conv, b_ig, b_fg, g_ml, lam_q1, lam_k1, lam_q2, lam_k2, g_diff, w_br_sb, w_br_df, w_br_ml, w_out, w_pq, sub_keys, peer_u, peer_v):
    bsz, seq, d = x_prompt.shape
    nb, ts, _ = x_sample.shape
    depth = w_ada.shape[0]
    n_p, n_s = bsz * seq, nb * ts
    xp = x_prompt.reshape(n_p, d)
    xs = x_sample.reshape(n_s, d)
    mod = adaln_all(jnp.concatenate([c_prompt, c_sample], axis=0), w_ada, b_ada).reshape(depth, bsz + nb, 6, d)
    new_prompt, new_sample = [], []
    for l in range(depth):
        lam_init = 0.8 - 0.6 * math.exp(-0.3 * l)
        mods_p = [mod[l, :bsz, i][:, None, :] for i in range(6)]
        mods_s = [jnp.repeat(mod[l, bsz:, i], ts, axis=0)[None] for i in range(6)]
        lam_vecs = jnp.stack([lam_q1[l], lam_k1[l], lam_q2[l], lam_k2[l]])
        bf = lambda a: a.astype(BF16)
        weights = (g_norm1[l], g_norm2[l], g_diff[l], g_ml[l], *prep_inproj_weights(w_in[l]), bf(w_br_sb[l]), bf(w_br_df[l]),
                   bf(w_br_ml[l]), bf(w_out[l]), bf(w_pq[l]), bf(sub_keys[l]), bf(peer_u[l]), bf(peer_v[l].T))
        conv_w = (w_conv[l], b_conv[l], b_ig[l], b_fg[l])

        def attend_prompt(p):
            sb_o = sb_attention(p["sb_q"], p["sb_k"], p["sb_v"], batch=bsz, tq=ATTN_TILE)
            df_o = df_attention(p["df_q"], p["df_k"], p["df_v"], lam_vecs, lam_init=lam_init, batch=bsz, tq=ATTN_TILE)
            zeros = lambda *shape: jnp.zeros((bsz,) + shape, F32)
            ml_h, *state = mlstm(p["ml_qk"], p["ml_v"], p["ml_if"], zeros(SUBLANES, 2 * ML_WIDTH), zeros(ML_HEADS, ML_DH, ML_DH),
                                 zeros(ML_HEADS, ML_DH), zeros(ML_HEADS), *conv_w, batch=bsz, valid=ML_CHUNK)
            return sb_o, df_o, ml_h, state

        def attend_sample(p):
            per_seq = lambda a, rows: a.reshape(nb, rows, a.shape[-1] * ts // rows)
            sb_o = sb_attention_sample(page_table, per_seq(p["sb_q"], ts), per_seq(p["sb_k"], ts), per_seq(p["sb_v"], ts),
                                       cache_sb_k, cache_sb_v, layer=l)
            df_o = df_attention_sample(page_table, per_seq(p["df_q"], ts), per_seq(p["df_k"], ts * DF_KV_HEADS),
                                       per_seq(p["df_v"], ts * DF_KV_HEADS), cache_df_k, cache_df_v, lam_vecs,
                                       layer=l, lam_init=lam_init)
            chunked = lambda a: jnp.pad(per_seq(a, ts), ((0, 0), (0, ML_CHUNK - ts), (0, 0))).reshape(nb * ML_CHUNK, a.shape[-1])
            conv8 = jnp.pad(state_ml_conv[l], ((0, 0), (SUBLANES - (ML_CONV - 1), 0), (0, 0)))
            ml_h, *state = mlstm(chunked(p["ml_qk"]), chunked(p["ml_v"]), chunked(p["ml_if"]), conv8, state_ml_c[l], state_ml_n[l],
                                 state_ml_m[l], *conv_w, batch=nb, valid=ts)
            ml_h = ml_h.reshape(nb, ML_CHUNK, ML_WIDTH)[:, :ts].reshape(n_s, ML_WIDTH)
            return sb_o.reshape(n_s, SB_WIDTH), df_o.reshape(n_s, DF_WIDTH), ml_h, state

        xp, p, state = _token_stage(xp, mods_p, attend_prompt, lam_init, weights, tm=TOKEN_TILE, tiles_per_row=seq // TOKEN_TILE)
        conv_p = p["ml_qk"].reshape(bsz, seq, 2 * ML_WIDTH)[:, seq - (ML_CONV - 1):]
        new_prompt.append((p["sb_k"].reshape(bsz, seq, SB_HEADS, SB_DH), p["sb_v"].reshape(bsz, seq, SB_HEADS, SB_DH),
                           p["df_k"].reshape(bsz, seq, DF_KV_HEADS, 2 * DF_DH), p["df_v"].reshape(bsz, seq, DF_KV_HEADS, DF_VDIM),
                           *state, conv_p))
        xs, p, state = _token_stage(xs, mods_s, attend_sample, lam_init, weights, tm=n_s, tiles_per_row=1)
        window = jnp.concatenate([state_ml_conv[l], p["ml_qk"].reshape(nb, ts, 2 * ML_WIDTH)], axis=1)
        new_sample.append((p["sb_k"].reshape(nb, ts, SB_HEADS, SB_DH), p["sb_v"].reshape(nb, ts, SB_HEADS, SB_DH),
                           p["df_k"].reshape(nb, ts, DF_KV_HEADS, 2 * DF_DH), p["df_v"].reshape(nb, ts, DF_KV_HEADS, DF_VDIM),
                           *state, window[:, ts:]))
    y_prompt = final_norm(xp, g_final, tm=TOKEN_TILE).reshape(bsz, seq, d)
    y_sample = final_norm(xs, g_final, tm=n_s).reshape(nb, ts, d)
    stack = lambda group: [jnp.stack(z) for z in zip(*group)]
    return (y_prompt, y_sample, *stack(new_prompt), *stack(new_sample))
```

```python
import functools
import math

import jax
import jax.numpy as jnp
from jax import lax
from jax.experimental import pallas as pl
from jax.experimental.pallas import tpu as pltpu

F32 = jnp.float32
BF16 = jnp.bfloat16

D_MODEL = 1024
SB_HEADS, SB_DH = 8, 64
SB_WIDTH = SB_HEADS * SB_DH
DF_HEADS, DF_KV_HEADS, DF_DH = 8, 4, 64
DF_VDIM = 2 * DF_DH
DF_WIDTH = DF_HEADS * DF_VDIM
ML_HEADS, ML_DH = 4, 128
ML_WIDTH = ML_HEADS * ML_DH
ML_CONV = 4
ML_CHUNK = 128
PK_HEADS, PK_NKEYS, PK_QDIM, PK_TOPK = 8, 128, 256, 16
PK_HALF = PK_QDIM // 2
PK_EXPERTS = PK_NKEYS * PK_NKEYS
EPS = 1e-6
NEG = -1e30
LOG2E = math.log2(math.e)

LANES = 128
SUBLANES = 8
VMEM_LIMIT = 56 * 1024 * 1024

_OFF_ML_I = 3 * SB_WIDTH + DF_HEADS * 2 * DF_DH + DF_KV_HEADS * 2 * DF_DH + DF_KV_HEADS * DF_VDIM + 2 * ML_WIDTH + 2 * ML_WIDTH
_OFF_GATES = _OFF_ML_I + 2 * ML_HEADS
_MAIN_OUTS = (("sb_q", SB_WIDTH), ("sb_k", SB_WIDTH), ("sb_v", SB_WIDTH), ("df_q", DF_HEADS * 2 * DF_DH),
              ("df_k", DF_KV_HEADS * 2 * DF_DH), ("df_v", DF_KV_HEADS * DF_VDIM), ("ml_qk", 2 * ML_WIDTH),
              ("ml_v", ML_WIDTH), ("ml_o", ML_WIDTH), ("gates", 3 * D_MODEL))
_MAIN_WIDTH = sum(w for _, w in _MAIN_OUTS)


def _params(*sem):
    return pltpu.CompilerParams(dimension_semantics=sem, vmem_limit_bytes=VMEM_LIMIT)


def _softplus(z):
    return jnp.maximum(z, 0.0) + jnp.log1p(jnp.exp(-jnp.abs(z)))


def _sigmoid(z):
    return 1.0 / (1.0 + jnp.exp(-z))


def _rms(x):
    return x * lax.rsqrt(jnp.mean(x * x, axis=-1, keepdims=True) + EPS)


def _modulate(x, g, shift, scale):
    return (_rms(x) * g) * (1.0 + scale) + shift


def _mod_spec(mod, tm, tiles_per_row):
    return pl.BlockSpec((None, mod.shape[1], mod.shape[2]), lambda i: (i // tiles_per_row, 0, 0))


def _ada_kernel(c_ref, w_ref, b_ref, o_ref):
    c = c_ref[...]
    a = c * _sigmoid(c)
    o_ref[...] = jnp.dot(a, w_ref[...], preferred_element_type=F32, precision=lax.Precision.HIGHEST) + b_ref[...]


def adaln_all(c_all, w_ada, b_ada):
    depth, d, d6 = w_ada.shape
    r = c_all.shape[0]
    tn = 1024
    return pl.pallas_call(
        _ada_kernel,
        grid=(depth, d6 // tn),
        in_specs=[pl.BlockSpec((r, d), lambda l, j: (0, 0)),
                  pl.BlockSpec((None, d, tn), lambda l, j: (l, 0, j)),
                  pl.BlockSpec((None, 1, tn), lambda l, j: (l, 0, j))],
        out_specs=pl.BlockSpec((None, r, tn), lambda l, j: (l, 0, j)),
        out_shape=jax.ShapeDtypeStruct((depth, r, d6), F32),
        compiler_params=_params("arbitrary", "arbitrary"),
        name="adaln",
    )(c_all, w_ada, b_ada.reshape(depth, 1, d6))


def _inproj_kernel(x_ref, g_ref, sh_ref, sc_ref, wm_ref, wif_ref, *out_refs):
    h = _modulate(x_ref[...], g_ref[...], sh_ref[...], sc_ref[...])
    hb = h.astype(BF16)
    off = 0
    for (_, width), o_ref in zip(_MAIN_OUTS, out_refs[:-1]):
        o_ref[...] = jnp.dot(hb, wm_ref[:, off:off + width], preferred_element_type=F32)
        off += width
    out_refs[-1][...] = jnp.dot(h, wif_ref[...], preferred_element_type=F32, precision=lax.Precision.HIGHEST)


def prep_inproj_weights(w_in_l):
    w_main = jnp.concatenate([w_in_l[:, :_OFF_ML_I], w_in_l[:, _OFF_GATES:]], axis=1).astype(BF16)
    w_if = jnp.pad(w_in_l[:, _OFF_ML_I:_OFF_GATES], ((0, 0), (0, LANES - 2 * ML_HEADS)))
    return w_main, w_if


def inproj(x, g, shift, scale, w_main, w_if, *, tm, tiles_per_row):
    n, d = x.shape
    row = lambda width: pl.BlockSpec((tm, width), lambda i: (i, 0))
    const = lambda shape: pl.BlockSpec(shape, lambda i: (0,) * len(shape), pipeline_mode=pl.Buffered(1))
    outs = pl.pallas_call(
        _inproj_kernel,
        grid=(n // tm,),
        in_specs=[row(d), const((1, d)), _mod_spec(shift, tm, tiles_per_row), _mod_spec(scale, tm, tiles_per_row),
                  const(w_main.shape), const(w_if.shape)],
        out_specs=[row(w) for _, w in _MAIN_OUTS] + [row(LANES)],
        out_shape=[jax.ShapeDtypeStruct((n, w), F32) for _, w in _MAIN_OUTS] + [jax.ShapeDtypeStruct((n, LANES), F32)],
        compiler_params=_params("arbitrary"),
        name="inproj",
    )(x, g.reshape(1, d), shift, scale, w_main, w_if)
    res = {name: o for (name, _), o in zip(_MAIN_OUTS, outs[:-1])}
    res["ml_if"] = outs[-1]
    return res


def _merge_kernel(x_ref, ga_ref, sb_ref, df_ref, mlh_ref, mlo_ref, gt_ref, gdf_ref, gml_ref,
                  wsb_ref, wdf_ref, wml_ref, wout_ref, o_ref, *, lam_init):
    d = x_ref.shape[-1]
    sb = jnp.dot(sb_ref[...].astype(BF16), wsb_ref[...], preferred_element_type=F32)
    gdf = gdf_ref[...] * (1.0 - lam_init)
    dfn = [(_rms(df_ref[:, h * DF_VDIM:(h + 1) * DF_VDIM]) * gdf).astype(BF16) for h in range(DF_HEADS)]
    df = jnp.dot(jnp.concatenate(dfn, axis=1), wdf_ref[...], preferred_element_type=F32)
    mln = []
    for h in range(ML_HEADS):
        s = slice(h * ML_DH, (h + 1) * ML_DH)
        mln.append((_rms(mlh_ref[:, s]) * gml_ref[:, s] * _sigmoid(mlo_ref[:, s])).astype(BF16))
    ml = jnp.dot(jnp.concatenate(mln, axis=1), wml_ref[...], preferred_element_type=F32)
    merged = (_sigmoid(gt_ref[:, 0:d]) * sb + _sigmoid(gt_ref[:, d:2 * d]) * df
              + _sigmoid(gt_ref[:, 2 * d:3 * d]) * ml)
    y = jnp.dot(merged.astype(BF16), wout_ref[...], preferred_element_type=F32)
    o_ref[...] = x_ref[...] + ga_ref[...] * y


def merge(x, ga, sb_o, df_o, ml_h, ml_o, gates, g_diff, g_ml, wsb, wdf, wml, wout, *, lam_init, tm, tiles_per_row):
    n, d = x.shape
    row = lambda width: pl.BlockSpec((tm, width), lambda i: (i, 0))
    const = lambda shape: pl.BlockSpec(shape, lambda i: (0,) * len(shape))
    return pl.pallas_call(
        functools.partial(_merge_kernel, lam_init=lam_init),
        grid=(n // tm,),
        in_specs=[row(d), _mod_spec(ga, tm, tiles_per_row), row(SB_WIDTH), row(DF_WIDTH), row(ML_WIDTH), row(ML_WIDTH),
                  row(3 * d), const((1, DF_VDIM)), const((1, ML_WIDTH)),
                  const(wsb.shape), const(wdf.shape), const(wml.shape), const(wout.shape)],
        out_specs=row(d),
        out_shape=jax.ShapeDtypeStruct((n, d), F32),
        compiler_params=_params("arbitrary"),
        name="merge",
    )(x, ga, sb_o, df_o, ml_h, ml_o, gates, g_diff.reshape(1, DF_VDIM), g_ml.reshape(1, ML_WIDTH), wsb, wdf, wml, wout)


def _final_norm_kernel(x_ref, g_ref, o_ref):
    o_ref[...] = _rms(x_ref[...]) * g_ref[...]


def final_norm(x, g, *, tm):
    n, d = x.shape
    return pl.pallas_call(
        _final_norm_kernel,
        grid=(n // tm,),
        in_specs=[pl.BlockSpec((tm, d), lambda i: (i, 0)), pl.BlockSpec((1, d), lambda i: (0, 0))],
        out_specs=pl.BlockSpec((tm, d), lambda i: (i, 0)),
        out_shape=jax.ShapeDtypeStruct((n, d), F32),
        compiler_params=_params("arbitrary"),
        name="final_norm",
    )(x, g.reshape(1, d))


def _split_bf16(x):
    hi = x.astype(BF16)
    return hi, (x - hi.astype(F32)).astype(BF16)


def _sb_kernel(q_ref, k_ref, v_ref, o_ref, *, tq, tk):
    qi = pl.program_id(2)
    lane = lax.broadcasted_iota(jnp.int32, (tq, LANES), 1)
    q = q_ref[...] * (SB_DH ** -0.5)
    diff = lax.broadcasted_iota(jnp.int32, (tq, tk), 1) - lax.broadcasted_iota(jnp.int32, (tq, tk), 0)
    sr = lax.broadcasted_iota(jnp.int32, (tk, tk), 0)
    sc = lax.broadcasted_iota(jnp.int32, (tk, tk), 1)
    suffix = (sr > sc).astype(BF16)
    nt = (((1,), (1,)), ((), ()))
    n_heads = LANES // SB_DH
    q_heads = [jnp.where(lane // SB_DH == h, q, 0.0).astype(BF16) for h in range(n_heads)]

    def tile(j, carry, masked):
        start = pl.multiple_of(j * tk, tk)
        kt = k_ref[pl.ds(start, tk), :].astype(BF16)
        vt = v_ref[pl.ds(start, tk), :].astype(BF16)
        earlier = diff < qi * tq - j * tk
        out = []
        for qh, (acc, run) in zip(q_heads, carry):
            z2 = lax.dot_general(qh, kt, nt, preferred_element_type=F32) * LOG2E
            nz2 = -z2
            soft = jnp.log2(1.0 + jnp.exp2(jnp.minimum(z2, nz2)))
            log_keep = jnp.minimum(nz2, 0.0) - soft
            if masked:
                log_keep = jnp.where(earlier, log_keep, 0.0)
            within = jnp.dot(log_keep.astype(BF16), suffix, preferred_element_type=F32)
            w = jnp.exp2((z2 + log_keep) + within + run)
            if masked:
                w = jnp.where(earlier, w, 0.0)
            out.append((acc + jnp.dot(w.astype(BF16), vt, preferred_element_type=F32),
                        run + jnp.sum(log_keep, axis=-1, keepdims=True)))
        return tuple(out)

    carry = ((jnp.zeros((tq, LANES), F32), jnp.zeros((tq, 1), F32)),) * n_heads
    first = (qi * tq) // tk
    for r in reversed(range(tq // tk)):
        carry = tile(first + r, carry, True)
    carry = lax.fori_loop(0, first, lambda i, c: tile(first - 1 - i, c, False), carry)
    o_ref[...] = jnp.where(lane < SB_DH, carry[0][0], carry[1][0])


def sb_attention(q, k, v, *, batch, tq, tk):
    n, width = q.shape
    t = n // batch
    nq = t // tq
    kv = lambda a: a.reshape(batch, t, width)
    kv_spec = pl.BlockSpec((None, t, LANES), lambda b, hp, i: (b, 0, hp))
    q_spec = pl.BlockSpec((tq, LANES), lambda b, hp, i: (b * nq + i, hp))
    return pl.pallas_call(
        functools.partial(_sb_kernel, tq=tq, tk=tk),
        grid=(batch, width // LANES, nq),
        in_specs=[q_spec, kv_spec, kv_spec],
        out_specs=q_spec,
        out_shape=jax.ShapeDtypeStruct((n, width), F32),
        compiler_params=_params("arbitrary", "arbitrary", "arbitrary"),
        name="sb_attention",
    )(q, kv(k), kv(v))


def _lambda_value(lam_ref, lam_init):
    lv = lam_ref[...]
    s1 = jnp.sum(lv[0:1, :] * lv[1:2, :], axis=-1, keepdims=True)
    s2 = jnp.sum(lv[2:3, :] * lv[3:4, :], axis=-1, keepdims=True)
    return jnp.exp(s1) - jnp.exp(s2) + lam_init


def _df_kernel(slope_ref, q_ref, k_ref, v_ref, lam_ref, o_ref, *, tq, tk, lam_init):
    group = DF_HEADS // DF_KV_HEADS
    kvh = pl.program_id(1)
    qi = pl.program_id(2)
    lane = lax.broadcasted_iota(jnp.int32, (tq, LANES), 1)
    row = lax.broadcasted_iota(jnp.int32, (tq, tk), 0)
    col = lax.broadcasted_iota(jnp.int32, (tq, tk), 1)
    diff = (col - row).astype(F32)
    ones = jnp.ones((tk, LANES), BF16)
    nt = (((1,), (1,)), ((), ()))
    chains = []
    for g in range(group):
        slope = slope_ref[kvh * group + g]
        qg = q_ref[:, g * LANES:(g + 1) * LANES] * (DF_DH ** -0.5)
        rel = diff * slope
        for m in range(2):
            chains.append((slope, rel, jnp.where(lane // DF_DH == m, qg, 0.0).astype(BF16)))

    def tile(j, carry, masked):
        start = pl.multiple_of(j * tk, tk)
        kt = k_ref[pl.ds(start, tk), :].astype(BF16)
        vt = jnp.concatenate([v_ref[pl.ds(start, tk), :].astype(BF16), ones], axis=1)
        shift = (qi * tq - j * tk).astype(F32)
        out = []
        for (slope, rel, qm), (acc, mx) in zip(chains, carry):
            s = lax.dot_general(qm, kt, nt, preferred_element_type=F32) + rel
            if masked:
                s = jnp.where(diff <= shift, s, NEG)
            off = -slope * shift
            mx_new = jnp.maximum(mx, jnp.max(s, axis=-1, keepdims=True) + off)
            e = jnp.exp(s + (off - mx_new))
            out.append((jnp.exp(mx - mx_new) * acc + jnp.dot(e.astype(BF16), vt, preferred_element_type=F32), mx_new))
        return tuple(out)

    init = tuple((jnp.zeros((tq, 2 * LANES), F32), jnp.full((tq, 1), NEG, F32)) for _ in chains)
    n_full = (qi * tq) // tk
    carry = lax.fori_loop(0, n_full, lambda j, c: tile(j, c, False), init)
    carry = tile(n_full, carry, True)
    lam = _lambda_value(lam_ref, lam_init)
    for g in range(group):
        res = [carry[2 * g + m][0][:, :LANES] / carry[2 * g + m][0][:, LANES:] for m in range(2)]
        o_ref[:, g * LANES:(g + 1) * LANES] = res[0] - lam * res[1]


def alibi_slope_table():
    return jnp.array([2.0 ** (-8.0 * (h + 1) / DF_HEADS) for h in range(DF_HEADS)], dtype=F32)


def df_attention(q, k, v, lam_vecs, *, lam_init, batch, tq, tk):
    n, width = q.shape
    t = n // batch
    nq = t // tq
    kv = lambda a: a.reshape(batch, t, a.shape[-1])
    group = DF_HEADS // DF_KV_HEADS
    kv_spec = pl.BlockSpec((None, t, LANES), lambda b, h, i, s: (b, 0, h))
    q_spec = pl.BlockSpec((tq, group * LANES), lambda b, h, i, s: (b * nq + i, h))
    return pl.pallas_call(
        functools.partial(_df_kernel, tq=tq, tk=tk, lam_init=lam_init),
        grid_spec=pltpu.PrefetchScalarGridSpec(
            num_scalar_prefetch=1,
            grid=(batch, DF_KV_HEADS, nq),
            in_specs=[q_spec, kv_spec, kv_spec, pl.BlockSpec(lam_vecs.shape, lambda b, h, i, s: (0, 0))],
            out_specs=q_spec),
        out_shape=jax.ShapeDtypeStruct((n, width), F32),
        compiler_params=_params("arbitrary", "arbitrary", "arbitrary"),
        name="df_attention",
    )(alibi_slope_table(), q, kv(k), kv(v), lam_vecs)


def _mlstm_kernel(qk_ref, v_ref, if_ref, conv0_ref, c0_ref, n0_ref, m0_ref, wc_ref, bc_ref, bif_ref,
                  h_ref, c_ref, n_ref, m_ref, buf, cs, ns, ms, *, valid):
    L = ML_CHUNK
    ci = pl.program_id(1)

    @pl.when(ci == 0)
    def _():
        buf[0:SUBLANES, :] = conv0_ref[...]
        cs[...] = c0_ref[...]
        ns[...] = n0_ref[...]
        ms[...] = m0_ref[...]

    cur = qk_ref[...]
    buf[SUBLANES:SUBLANES + L, :] = cur
    acc = bc_ref[...]
    for j in range(ML_CONV):
        lo = SUBLANES - (ML_CONV - 1) + j
        acc = acc + buf[lo:lo + L, :] * wc_ref[j:j + 1, :]
    qk = acc * _sigmoid(acc)
    buf[0:SUBLANES, :] = cur[L - SUBLANES:L, :]

    row = lax.broadcasted_iota(jnp.int32, (L, LANES), 0)
    lane = lax.broadcasted_iota(jnp.int32, (L, LANES), 1)
    gt = if_ref[...] + bif_ref[...]
    is_valid = row < valid
    ig = jnp.where(is_valid, gt, NEG)
    lf = jnp.where(is_valid & (lane >= ML_HEADS) & (lane < 2 * ML_HEADS), -_softplus(-gt), 0.0)
    r2 = lax.broadcasted_iota(jnp.int32, (L, L), 0)
    c2 = lax.broadcasted_iota(jnp.int32, (L, L), 1)
    causal = c2 <= r2
    bcum = jnp.dot(causal.astype(F32), lf, preferred_element_type=F32, precision=lax.Precision.HIGHEST)
    ig_t = ig.T
    bcum_t = bcum.T
    nt = (((1,), (1,)), ((), ()))
    for h in range(ML_HEADS):
        b_col = bcum[:, ML_HEADS + h:ML_HEADS + h + 1]
        b_row = bcum_t[ML_HEADS + h:ML_HEADS + h + 1, :]
        ig_row = ig_t[h:h + 1, :]
        ig_col = ig[:, h:h + 1]
        m_run = ms[h:h + 1, 0:1]
        log_d = jnp.where(causal, b_col - b_row + ig_row, NEG)
        log_inter = b_col + m_run
        m_t = jnp.maximum(log_inter, jnp.max(log_d, axis=-1, keepdims=True))
        a_inter = jnp.exp(log_inter - m_t)
        dmat = jnp.exp(log_d - m_t)
        hs = slice(h * ML_DH, (h + 1) * ML_DH)
        qh = qk[:, hs]
        kh = qk[:, ML_WIDTH + h * ML_DH:ML_WIDTH + (h + 1) * ML_DH] * (ML_DH ** -0.5)
        qb, kb, vb = qh.astype(BF16), kh.astype(BF16), v_ref[:, hs].astype(BF16)
        scores = lax.dot_general(qb, kb, nt, preferred_element_type=F32) * dmat
        c_mat = cs[h]
        n_vec = ns[h:h + 1, :]
        num = (a_inter * jnp.dot(qb, c_mat.astype(BF16), preferred_element_type=F32)
               + jnp.dot(scores.astype(BF16), vb, preferred_element_type=F32))
        den = a_inter * jnp.sum(qh * n_vec, axis=-1, keepdims=True) + jnp.sum(scores, axis=-1, keepdims=True)
        h_ref[:, hs] = num / jnp.maximum(jnp.abs(den), jnp.exp(-m_t))
        m_new = m_t[L - 1:L, :]
        b_last = b_col[L - 1:L, :]
        w_end = jnp.exp(b_last - b_col + ig_col - m_new)
        decay = jnp.exp(b_last + m_run - m_new)
        kw = kh * w_end
        cs[h] = decay * c_mat + jnp.dot(kw.T.astype(BF16), vb, preferred_element_type=F32)
        ns[h:h + 1, :] = decay * n_vec + jnp.sum(kw, axis=0, keepdims=True)
        ms[h:h + 1, :] = jnp.broadcast_to(m_new, (1, LANES))

    @pl.when(ci == pl.num_programs(1) - 1)
    def _():
        c_ref[...] = cs[...]
        n_ref[...] = ns[...]
        m_ref[...] = ms[...]


def mlstm(ml_qk, ml_v, ml_if, conv8, c0, n0, m0, w_conv, b_conv, b_ig, b_fg, *, batch, valid):
    n_tok = ml_qk.shape[0]
    L = ML_CHUNK
    nc = n_tok // batch // L
    pad_rows = lambda a: jnp.pad(a, ((0, 0), (0, SUBLANES - ML_HEADS), (0, 0)))
    n0p = pad_rows(n0)
    m0p = pad_rows(jnp.broadcast_to(m0[:, :, None], (batch, ML_HEADS, LANES)))
    bias_if = jnp.pad(jnp.concatenate([b_ig, b_fg]), (0, LANES - 2 * ML_HEADS)).reshape(1, LANES)
    tok = lambda width: pl.BlockSpec((L, width), lambda b, c: (b * nc + c, 0))
    per_b = lambda *shape: pl.BlockSpec((None,) + shape, lambda b, c: (b,) + (0,) * len(shape))
    const = lambda *shape: pl.BlockSpec(shape, lambda b, c: (0,) * len(shape))
    h, c_out, n_out, m_out = pl.pallas_call(
        functools.partial(_mlstm_kernel, valid=valid),
        grid=(batch, nc),
        in_specs=[tok(2 * ML_WIDTH), tok(ML_WIDTH), tok(LANES), per_b(SUBLANES, 2 * ML_WIDTH),
                  per_b(ML_HEADS, ML_DH, ML_DH), per_b(SUBLANES, LANES), per_b(SUBLANES, LANES),
                  const(ML_CONV, 2 * ML_WIDTH), const(1, 2 * ML_WIDTH), const(1, LANES)],
        out_specs=[tok(ML_WIDTH), per_b(ML_HEADS, ML_DH, ML_DH), per_b(SUBLANES, LANES), per_b(SUBLANES, LANES)],
        out_shape=[jax.ShapeDtypeStruct((n_tok, ML_WIDTH), F32), jax.ShapeDtypeStruct((batch, ML_HEADS, ML_DH, ML_DH), F32),
                   jax.ShapeDtypeStruct((batch, SUBLANES, LANES), F32), jax.ShapeDtypeStruct((batch, SUBLANES, LANES), F32)],
        scratch_shapes=[pltpu.VMEM((SUBLANES + L, 2 * ML_WIDTH), F32), pltpu.VMEM((ML_HEADS, ML_DH, ML_DH), F32),
                        pltpu.VMEM((SUBLANES, LANES), F32), pltpu.VMEM((SUBLANES, LANES), F32)],
        compiler_params=_params("arbitrary", "arbitrary"),
        name="mlstm",
    )(ml_qk, ml_v, ml_if, conv8, c0, n0p, m0p, w_conv, b_conv.reshape(1, -1), bias_if)
    return h, c_out, n_out[:, :ML_HEADS], m_out[:, :ML_HEADS, 0]


def _top16(s):
    rows = lax.broadcasted_iota(jnp.int32, (PK_TOPK, s.shape[1]), 0)
    work = s
    rank = jnp.full(s.shape, float(PK_TOPK), F32)
    top = jnp.zeros((PK_TOPK, s.shape[1]), F32)
    for k in range(PK_TOPK):
        m = jnp.max(work, axis=0, keepdims=True)
        sel = work == m
        rank = jnp.where(sel, float(k), rank)
        work = jnp.where(sel, NEG, work)
        top = jnp.where(rows == k, m, top)
    return top, rank


def _route_kernel(x_ref, g_ref, sh_ref, sc_ref, wpq_ref, keys_ref, xt_ref, r2_ref, nb_ref, e1_ref, e2_ref):
    tm = x_ref.shape[0]
    x2 = _modulate(x_ref[...], g_ref[...], sh_ref[...], sc_ref[...])
    xt_ref[...] = x2.T.astype(BF16)
    q = jnp.dot(x2.astype(BF16), wpq_ref[...], preferred_element_type=F32)
    nt = (((1,), (1,)), ((), ()))
    row8 = lax.broadcasted_iota(jnp.int32, (SUBLANES, tm), 0)
    for h in range(PK_HEADS):
        s, top, rank = [], [], []
        for p in range(2):
            lo = (2 * h + p) * PK_HALF
            sp = lax.dot_general(keys_ref[h, p], q[:, lo:lo + PK_HALF].astype(BF16), nt, preferred_element_type=F32)
            tp, rk = _top16(sp)
            s.append(sp)
            top.append(tp)
            rank.append(rk)
        blocks = [top[0][0:1, :] + top[1]]
        for a in range(1, PK_TOPK):
            width = PK_TOPK // (a + 1)
            blocks.append(jnp.where(row8 < width, top[0][a:a + 1, :] + top[1][0:SUBLANES, :], NEG))
        cand = jnp.concatenate(blocks, axis=0)
        work = cand
        for k in range(PK_TOPK):
            thr = jnp.max(work, axis=0, keepdims=True)
            work = jnp.where(work == thr, NEG, work)
        chosen = cand >= thr
        cmax = top[0][0:1, :] + top[1][0:1, :]
        z = jnp.sum(jnp.where(chosen, jnp.exp(cand - cmax), 0.0), axis=0, keepdims=True)
        chosen_f = chosen.astype(F32)
        nb_rows = jnp.zeros_like(rank[0])
        lo = 0
        for a in range(PK_TOPK):
            size = PK_TOPK if a == 0 else SUBLANES
            nb_a = jnp.sum(chosen_f[lo:lo + size, :], axis=0, keepdims=True)
            nb_rows = jnp.where(rank[0] == float(a), nb_a, nb_rows)
            lo += size
        r2_ref[h] = rank[1].astype(BF16)
        nb_ref[h] = nb_rows
        e1_ref[h] = jnp.exp(s[0] - top[0][0:1, :]) / z
        e2_ref[h] = jnp.exp(s[1] - top[1][0:1, :]).astype(BF16)


def peer_route(x, g, shift, scale, w_pq, sub_keys, *, tm, tiles_per_row):
    n, d = x.shape
    const = lambda shape: pl.BlockSpec(shape, lambda i: (0,) * len(shape))
    lanes = pl.BlockSpec((PK_HEADS, PK_NKEYS, tm), lambda i: (0, 0, i))
    routed = lambda dtype: jax.ShapeDtypeStruct((PK_HEADS, PK_NKEYS, n), dtype)
    return pl.pallas_call(
        _route_kernel,
        grid=(n // tm,),
        in_specs=[pl.BlockSpec((tm, d), lambda i: (i, 0)), const((1, d)), _mod_spec(shift, tm, tiles_per_row),
                  _mod_spec(scale, tm, tiles_per_row), const(w_pq.shape), const(sub_keys.shape)],
        out_specs=[pl.BlockSpec((d, tm), lambda i: (0, i)), lanes, lanes, lanes, lanes],
        out_shape=[jax.ShapeDtypeStruct((d, n), BF16), routed(BF16), routed(F32), routed(F32), routed(BF16)],
        compiler_params=_params("arbitrary"),
        name="peer_route",
    )(x, g.reshape(1, d), shift, scale, w_pq, sub_keys)


def _gelu_tanh(x):
    return 0.5 * x * (1.0 + jnp.tanh(math.sqrt(2.0 / math.pi) * (x + 0.044715 * (x * x * x))))


BF16_ROWS = 16


def _expert_kernel(x_ref, ga_ref, xt_ref, r2_ref, nb_ref, e1_ref, e2_ref, u_ref, vt_ref, o_ref, acc, *, sub):
    c = pl.program_id(1)
    tm = xt_ref.shape[1]
    groups = PK_NKEYS // BF16_ROWS

    @pl.when(c == 0)
    def _():
        acc[...] = jnp.zeros_like(acc)

    xt = xt_ref[...]
    rank2 = [r2_ref[h].reshape(groups, BF16_ROWS, tm) for h in range(PK_HEADS)]
    gate2 = [e2_ref[h].reshape(groups, BF16_ROWS, tm) for h in range(PK_HEADS)]
    tile_row = lambda ref, h, e: jnp.broadcast_to(ref[h, e:e + 1, :], (BF16_ROWS, tm)).astype(BF16)[None]
    for s0 in range(0, u_ref.shape[0], sub):
        act = _gelu_tanh(jnp.dot(u_ref[s0:s0 + sub, :], xt, preferred_element_type=F32)).astype(BF16)
        slabs = []
        for e in range(s0 // PK_NKEYS, (s0 + sub) // PK_NKEYS):
            gate = None
            for h in range(PK_HEADS):
                term = jnp.where(rank2[h] < tile_row(nb_ref, h, e), gate2[h], jnp.zeros_like(gate2[h])) * tile_row(e1_ref, h, e)
                gate = term if gate is None else gate + term
            slabs.append(gate.reshape(PK_NKEYS, tm))
        weighted = jnp.concatenate(slabs, axis=0) * act
        acc[...] += jnp.dot(vt_ref[:, s0:s0 + sub], weighted, preferred_element_type=F32)

    @pl.when(c == pl.num_programs(1) - 1)
    def _():
        o_ref[...] = x_ref[...] + ga_ref[...] * acc[...].T


def peer_expert(x, ga, xt, r2, nb, e1c, e2, u, vt, *, tm, tiles_per_row, chunk=2048, sub=2048):
    n, d = x.shape
    n_exp = u.shape[0]
    e1_per_chunk = chunk // PK_NKEYS
    lanes = pl.BlockSpec((PK_HEADS, PK_NKEYS, tm), lambda i, c: (0, 0, i))
    by_e1 = pl.BlockSpec((PK_HEADS, e1_per_chunk, tm), lambda i, c: (0, c, i))
    mod = pl.BlockSpec((None, ga.shape[1], ga.shape[2]), lambda i, c: (i // tiles_per_row, 0, 0))
    return pl.pallas_call(
        functools.partial(_expert_kernel, sub=sub),
        grid=(n // tm, n_exp // chunk),
        in_specs=[pl.BlockSpec((tm, d), lambda i, c: (i, 0)), mod, pl.BlockSpec((d, tm), lambda i, c: (0, i)),
                  lanes, by_e1, by_e1, lanes,
                  pl.BlockSpec((chunk, d), lambda i, c: (c, 0)), pl.BlockSpec((d, chunk), lambda i, c: (0, c))],
        out_specs=pl.BlockSpec((tm, d), lambda i, c: (i, 0)),
        out_shape=jax.ShapeDtypeStruct((n, d), F32),
        scratch_shapes=[pltpu.VMEM((d, tm), F32)],
        compiler_params=_params("arbitrary", "arbitrary"),
        name="peer_expert",
    )(x, ga, xt, r2, nb, e1c, e2, u, vt)


PAGES_PER_STEP = 8


def _page_specs(n_pages, rows, width, layer, pps):
    return [pl.BlockSpec((None, None, rows, width),
                         lambda b, s, pt, k=k: (layer, pt[b, n_pages - 1 - (s * pps + k)], 0, 0)) for k in range(pps)]


def _sbs_kernel(pt_ref, qb_ref, kn_ref, vn_ref, sfx_ref, *refs, pps):
    k_refs, v_refs, o_ref, acc, run = refs[:pps], refs[pps:2 * pps], refs[2 * pps], refs[2 * pps + 1], refs[2 * pps + 2]
    s = pl.program_id(1)
    n_q, page = run.shape
    t_new = o_ref.shape[0]
    qb = (qb_ref[...] * (SB_DH ** -0.5)).astype(BF16)
    nt = (((1,), (1,)), ((), ()))

    def process(pages, mask):
        zs = [jnp.dot(qb, k_ref[...].astype(BF16), preferred_element_type=F32) for k_ref, _ in pages]
        sps = [_softplus(z) for z in zs]
        log_keeps = [-sp if mask is None else jnp.where(mask, -sp, 0.0) for sp in sps]
        hi, lo = _split_bf16(jnp.concatenate(log_keeps, axis=0))
        suffix = sfx_ref[...]
        within = jnp.dot(hi, suffix, preferred_element_type=F32) + jnp.dot(lo, suffix, preferred_element_type=F32)
        running = run[:, 0:1]
        out = acc[...]
        for i, (z, sp, log_keep, (_, v_ref)) in enumerate(zip(zs, sps, log_keeps, pages)):
            w = jnp.exp((z - sp) + within[i * n_q:(i + 1) * n_q, :] + running)
            if mask is not None:
                w = jnp.where(mask, w, 0.0)
            out = out + lax.dot_general(w.astype(BF16), v_ref[...].astype(BF16), nt, preferred_element_type=F32)
            running = running + jnp.sum(log_keep, axis=1, keepdims=True)
        acc[...] = out
        run[...] = jnp.broadcast_to(running, run.shape)

    @pl.when(s == 0)
    def _():
        acc[...] = jnp.zeros_like(acc)
        run[...] = jnp.zeros_like(run)
        pos = lax.broadcasted_iota(jnp.int32, (n_q, page), 1)
        t_row = lax.broadcasted_iota(jnp.int32, (n_q, page), 0) // SB_HEADS
        process([(kn_ref, vn_ref)], pos < t_row)

    process(list(zip(k_refs, v_refs)), None)

    @pl.when(s == pl.num_programs(1) - 1)
    def _():
        h_row = lax.broadcasted_iota(jnp.int32, (SB_HEADS, SB_WIDTH), 0)
        h_col = lax.broadcasted_iota(jnp.int32, (SB_HEADS, SB_WIDTH), 1) // SB_DH
        own = acc[...].reshape(t_new, SB_HEADS, SB_WIDTH) * (h_row == h_col).astype(F32)
        o_ref[...] = jnp.sum(own, axis=1)


def sb_attention_sample(page_table, q, k_new, v_new, cache_k, cache_v, *, layer, pps=PAGES_PER_STEP):
    nb, t_new, width = q.shape
    n_pages = page_table.shape[1]
    pps = math.gcd(pps, n_pages)
    depth, n_pool, page, heads, dh = cache_k.shape
    n_q = t_new * heads
    eye = jnp.eye(heads, dtype=F32)[None, None, :, :, None]
    qb = (q.reshape(nb, t_new, heads, 1, dh) * eye).reshape(nb, n_q, width)
    new_page = lambda a: jnp.pad(a.transpose(0, 2, 1), ((0, 0), (0, 0), (0, page - t_new)))
    paged = lambda a: a.transpose(0, 1, 3, 4, 2).reshape(depth, n_pool, width, page)
    per_b = lambda *shape: pl.BlockSpec((None,) + shape, lambda b, s, pt: (b,) + (0,) * len(shape))
    key = jnp.arange(page)
    suffix = (key[:, None] > key[None, :]).astype(BF16)
    return pl.pallas_call(
        functools.partial(_sbs_kernel, pps=pps),
        grid_spec=pltpu.PrefetchScalarGridSpec(
            num_scalar_prefetch=1,
            grid=(nb, n_pages // pps),
            in_specs=[per_b(n_q, width), per_b(width, page), per_b(width, page),
                      pl.BlockSpec(suffix.shape, lambda b, s, pt: (0, 0))]
            + _page_specs(n_pages, width, page, layer, pps) + _page_specs(n_pages, width, page, layer, pps),
            out_specs=per_b(t_new, width),
            scratch_shapes=[pltpu.VMEM((n_q, width), F32), pltpu.VMEM((n_q, page), F32)]),
        out_shape=jax.ShapeDtypeStruct((nb, t_new, width), F32),
        compiler_params=_params("arbitrary", "arbitrary"),
        name="sb_attention_sample",
    )(page_table, qb, new_page(k_new), new_page(v_new), suffix, *([paged(cache_k)] * pps), *([paged(cache_v)] * pps))


def _dfs_kernel(pt_ref, qc_ref, kn_ref, vn_ref, col_ref, lam_ref, *refs, pps, n_pages, lam_init):
    k_refs, v_refs, o_ref, acc, mx = refs[:pps], refs[pps:2 * pps], refs[2 * pps], refs[2 * pps + 1], refs[2 * pps + 2]
    s = pl.program_id(1)
    rows = kn_ref.shape[0]
    page = rows // DF_KV_HEADS
    half = LANES // 4
    r = lax.broadcasted_iota(jnp.int32, (rows, LANES), 0)
    c = lax.broadcasted_iota(jnp.int32, (rows, LANES), 1)
    group = DF_HEADS // DF_KV_HEADS
    same_kv = (r % DF_KV_HEADS == (c // group) % DF_KV_HEADS) & (c < 2 * half)
    slope = col_ref[0:1, :]
    t_col = col_ref[1:2, :]
    tok = (r // DF_KV_HEADS).astype(F32)
    slope_tok = slope * tok
    qc = (qc_ref[...] * (DF_DH ** -0.5)).astype(BF16)

    def process(pages, mask):
        scores = []
        for k_ref, _, base in pages:
            z = jnp.dot(k_ref[...].astype(BF16), qc, preferred_element_type=F32)
            scores.append(jnp.where(mask, z + slope_tok + slope * (base - t_col), NEG))
        mx_old = mx[0:1, :]
        mx_new = mx_old
        for sb in scores:
            mx_new = jnp.maximum(mx_new, jnp.max(sb, axis=0, keepdims=True))
        e = jnp.concatenate([jnp.exp(sb - mx_new).astype(BF16) for sb in scores], axis=0)
        vt = jnp.concatenate([v_ref[...].T.astype(BF16) for _, v_ref, _ in pages], axis=1)
        vext = jnp.concatenate([vt, jnp.ones_like(vt)], axis=0)
        acc[...] = jnp.exp(mx_old - mx_new) * acc[...] + jnp.dot(vext, e, preferred_element_type=F32)
        mx[...] = jnp.broadcast_to(mx_new, mx.shape)

    @pl.when(s == 0)
    def _():
        acc[...] = jnp.zeros_like(acc)
        mx[...] = jnp.full_like(mx, NEG)
        process([(kn_ref, vn_ref, 0.0)], same_kv & (tok <= t_col))

    first_page = n_pages - 1 - s * pps
    process([(k_refs[k], v_refs[k], ((first_page - k - n_pages) * page).astype(F32)) for k in range(pps)], same_kv)

    @pl.when(s == pl.num_programs(1) - 1)
    def _():
        res = acc[0:LANES, :] / acc[LANES:2 * LANES, :]
        o_ref[...] = res - _lambda_value(lam_ref, lam_init) * pltpu.roll(res, LANES - half, axis=1)


def df_attention_sample(page_table, q, k_new, v_new, cache_k, cache_v, lam_vecs, *, layer, lam_init, pps=PAGES_PER_STEP):
    nb, t_new, _ = q.shape
    n_pages = page_table.shape[1]
    pps = math.gcd(pps, n_pages)
    depth, n_pool, page, kvh, width = cache_k.shape
    rows = page * kvh
    group = DF_HEADS // DF_KV_HEADS
    half = LANES // 4
    assert t_new * kvh * group == half and width == LANES
    qt = q.reshape(nb, t_new * kvh * group, width).transpose(0, 2, 1)
    d_half = (jnp.arange(width) // DF_DH)[None, :, None]
    qc = jnp.concatenate([jnp.where(d_half == 0, qt, 0.0), jnp.where(d_half == 1, qt, 0.0),
                          jnp.zeros((nb, width, LANES - 2 * half), F32)], axis=2)
    col = jnp.arange(LANES)
    head_of_col = ((col // group) % kvh) * group + col % group
    slope_col = jnp.where(col < 2 * half, alibi_slope_table()[head_of_col], 0.0)
    t_of_col = ((col % half) // (kvh * group)).astype(F32)
    col_info = jnp.zeros((SUBLANES, LANES), F32).at[0].set(slope_col).at[1].set(t_of_col)
    pad_page = lambda a: jnp.pad(a, ((0, 0), (0, rows - a.shape[1]), (0, 0)))
    paged = lambda a: a.reshape(depth, n_pool, rows, width)
    per_b = lambda *shape: pl.BlockSpec((None,) + shape, lambda b, s, pt: (b,) + (0,) * len(shape))
    const = lambda *shape: pl.BlockSpec(shape, lambda b, s, pt: (0,) * len(shape))
    out = pl.pallas_call(
        functools.partial(_dfs_kernel, pps=pps, n_pages=n_pages, lam_init=lam_init),
        grid_spec=pltpu.PrefetchScalarGridSpec(
            num_scalar_prefetch=1,
            grid=(nb, n_pages // pps),
            in_specs=[per_b(width, LANES), per_b(rows, width), per_b(rows, width), const(SUBLANES, LANES), const(*lam_vecs.shape)]
            + _page_specs(n_pages, rows, width, layer, pps) + _page_specs(n_pages, rows, width, layer, pps),
            out_specs=per_b(width, LANES),
            scratch_shapes=[pltpu.VMEM((2 * LANES, LANES), F32), pltpu.VMEM((SUBLANES, LANES), F32)]),
        out_shape=jax.ShapeDtypeStruct((nb, width, LANES), F32),
        compiler_params=_params("arbitrary", "arbitrary"),
        name="df_attention_sample",
    )(page_table, qc, pad_page(k_new), pad_page(v_new), col_info, lam_vecs, *([paged(cache_k)] * pps), *([paged(cache_v)] * pps))
    return out[:, :, :half].transpose(0, 2, 1).reshape(nb, t_new, kvh * group * width)


TOKEN_TILE = 256
ATTN_TILE = 256


def _token_stage(x, mods, attend, lam_init, weights, *, tm, tiles_per_row):
    (g1, g2, g_diff_l, g_ml_l, w_main, w_if, wsb, wdf, wml, wout, wpq, keys, u, vt) = weights
    p = inproj(x, g1, mods[0], mods[1], w_main, w_if, tm=tm, tiles_per_row=tiles_per_row)
    sb_o, df_o, ml_h, state = attend(p)
    x = merge(x, mods[2], sb_o, df_o, ml_h, p["ml_o"], p["gates"], g_diff_l, g_ml_l, wsb, wdf, wml, wout,
              lam_init=lam_init, tm=tm, tiles_per_row=tiles_per_row)
    routed = peer_route(x, g2, mods[3], mods[4], wpq, keys, tm=tm, tiles_per_row=tiles_per_row)
    wide = 2 if tiles_per_row % 2 == 0 else 1
    x = peer_expert(x, mods[5], *routed, u, vt, tm=wide * tm, tiles_per_row=tiles_per_row // wide)
    return x, p, state


def kernel(x_prompt, x_sample, cache_sb_k, cache_sb_v, cache_df_k, cache_df_v, state_ml_c, state_ml_n, state_ml_m, state_ml_conv, page_table, c_prompt, c_sample, w_ada, b_ada, g_norm1, g_norm2, g_final, w_in, w_conv, b_conv, b_ig, b_fg, g_ml, lam_q1, lam_k1, lam_q2, lam_k2, g_diff, w_br_sb, w_br_df, w_br_ml, w_out, w_pq, sub_keys, peer_u, peer_v):
    bsz, seq, d = x_prompt.shape
    nb, ts, _ = x_sample.shape
    depth = w_ada.shape[0]
    n_p, n_s = bsz * seq, nb * ts
    xp = x_prompt.reshape(n_p, d)
    xs = x_sample.reshape(n_s, d)
    mod = adaln_all(jnp.concatenate([c_prompt, c_sample], axis=0), w_ada, b_ada).reshape(depth, bsz + nb, 6, d)
    new_prompt, new_sample = [], []
    for l in range(depth):
        lam_init = 0.8 - 0.6 * math.exp(-0.3 * l)
        mods_p = [mod[l, :bsz, i][:, None, :] for i in range(6)]
        mods_s = [jnp.repeat(mod[l, bsz:, i], ts, axis=0)[None] for i in range(6)]
        lam_vecs = jnp.stack([lam_q1[l], lam_k1[l], lam_q2[l], lam_k2[l]])
        bf = lambda a: a.astype(BF16)
        weights = (g_norm1[l], g_norm2[l], g_diff[l], g_ml[l], *prep_inproj_weights(w_in[l]), bf(w_br_sb[l]), bf(w_br_df[l]),
                   bf(w_br_ml[l]), bf(w_out[l]), bf(w_pq[l]), bf(sub_keys[l]), bf(peer_u[l]), bf(peer_v[l].T))
        conv_w = (w_conv[l], b_conv[l], b_ig[l], b_fg[l])

        def attend_prompt(p):
            sb_o = sb_attention(p["sb_q"], p["sb_k"], p["sb_v"], batch=bsz, tq=min(2 * ATTN_TILE, seq), tk=ATTN_TILE)
            df_o = df_attention(p["df_q"], p["df_k"], p["df_v"], lam_vecs, lam_init=lam_init, batch=bsz, tq=ATTN_TILE,
                                tk=min(2 * ATTN_TILE, seq))
            zeros = lambda *shape: jnp.zeros((bsz,) + shape, F32)
            ml_h, *state = mlstm(p["ml_qk"], p["ml_v"], p["ml_if"], zeros(SUBLANES, 2 * ML_WIDTH), zeros(ML_HEADS, ML_DH, ML_DH),
                                 zeros(ML_HEADS, ML_DH), zeros(ML_HEADS), *conv_w, batch=bsz, valid=ML_CHUNK)
            return sb_o, df_o, ml_h, state

        def attend_sample(p):
            per_seq = lambda a, rows: a.reshape(nb, rows, a.shape[-1] * ts // rows)
            sb_o = sb_attention_sample(page_table, per_seq(p["sb_q"], ts), per_seq(p["sb_k"], ts), per_seq(p["sb_v"], ts),
                                       cache_sb_k, cache_sb_v, layer=l)
            df_o = df_attention_sample(page_table, per_seq(p["df_q"], ts), per_seq(p["df_k"], ts * DF_KV_HEADS),
                                       per_seq(p["df_v"], ts * DF_KV_HEADS), cache_df_k, cache_df_v, lam_vecs,
                                       layer=l, lam_init=lam_init)
            chunked = lambda a: jnp.pad(per_seq(a, ts), ((0, 0), (0, ML_CHUNK - ts), (0, 0))).reshape(nb * ML_CHUNK, a.shape[-1])
            conv8 = jnp.pad(state_ml_conv[l], ((0, 0), (SUBLANES - (ML_CONV - 1), 0), (0, 0)))
            ml_h, *state = mlstm(chunked(p["ml_qk"]), chunked(p["ml_v"]), chunked(p["ml_if"]), conv8, state_ml_c[l], state_ml_n[l],
                                 state_ml_m[l], *conv_w, batch=nb, valid=ts)
            ml_h = ml_h.reshape(nb, ML_CHUNK, ML_WIDTH)[:, :ts].reshape(n_s, ML_WIDTH)
            return sb_o.reshape(n_s, SB_WIDTH), df_o.reshape(n_s, DF_WIDTH), ml_h, state

        xp, p, state = _token_stage(xp, mods_p, attend_prompt, lam_init, weights, tm=TOKEN_TILE, tiles_per_row=seq // TOKEN_TILE)
        conv_p = p["ml_qk"].reshape(bsz, seq, 2 * ML_WIDTH)[:, seq - (ML_CONV - 1):]
        new_prompt.append((p["sb_k"].reshape(bsz, seq, SB_HEADS, SB_DH), p["sb_v"].reshape(bsz, seq, SB_HEADS, SB_DH),
                           p["df_k"].reshape(bsz, seq, DF_KV_HEADS, 2 * DF_DH), p["df_v"].reshape(bsz, seq, DF_KV_HEADS, DF_VDIM),
                           *state, conv_p))
        xs, p, state = _token_stage(xs, mods_s, attend_sample, lam_init, weights, tm=n_s, tiles_per_row=1)
        window = jnp.concatenate([state_ml_conv[l], p["ml_qk"].reshape(nb, ts, 2 * ML_WIDTH)], axis=1)
        new_sample.append((p["sb_k"].reshape(nb, ts, SB_HEADS, SB_DH), p["sb_v"].reshape(nb, ts, SB_HEADS, SB_DH),
                           p["df_k"].reshape(nb, ts, DF_KV_HEADS, 2 * DF_DH), p["df_v"].reshape(nb, ts, DF_KV_HEADS, DF_VDIM),
                           *state, window[:, ts:]))
    y_prompt = final_norm(xp, g_final, tm=TOKEN_TILE).reshape(bsz, seq, d)
    y_sample = final_norm(xs, g_final, tm=n_s).reshape(nb, ts, d)
    stack = lambda group: [jnp.stack(z) for z in zip(*group)]
    return (y_prompt, y_sample, *stack(new_prompt), *stack(new_sample))
```

```python
import functools
import math

import jax
import jax.numpy as jnp
from jax import lax
from jax.experimental import pallas as pl
from jax.experimental.pallas import tpu as pltpu

F32 = jnp.float32
BF16 = jnp.bfloat16

D_MODEL = 1024
SB_HEADS, SB_DH = 8, 64
SB_WIDTH = SB_HEADS * SB_DH
DF_HEADS, DF_KV_HEADS, DF_DH = 8, 4, 64
DF_VDIM = 2 * DF_DH
DF_WIDTH = DF_HEADS * DF_VDIM
ML_HEADS, ML_DH = 4, 128
ML_WIDTH = ML_HEADS * ML_DH
ML_CONV = 4
ML_CHUNK = 128
PK_HEADS, PK_NKEYS, PK_QDIM, PK_TOPK = 8, 128, 256, 16
PK_HALF = PK_QDIM // 2
PK_EXPERTS = PK_NKEYS * PK_NKEYS
EPS = 1e-6
NEG = -1e30
LOG2E = math.log2(math.e)

LANES = 128
SUBLANES = 8
VMEM_LIMIT = 56 * 1024 * 1024

_OFF_ML_I = 3 * SB_WIDTH + DF_HEADS * 2 * DF_DH + DF_KV_HEADS * 2 * DF_DH + DF_KV_HEADS * DF_VDIM + 2 * ML_WIDTH + 2 * ML_WIDTH
_OFF_GATES = _OFF_ML_I + 2 * ML_HEADS
_MAIN_OUTS = (("sb_q", SB_WIDTH), ("sb_k", SB_WIDTH), ("sb_v", SB_WIDTH), ("df_q", DF_HEADS * 2 * DF_DH),
              ("df_k", DF_KV_HEADS * 2 * DF_DH), ("df_v", DF_KV_HEADS * DF_VDIM), ("ml_qk", 2 * ML_WIDTH),
              ("ml_v", ML_WIDTH), ("ml_o", ML_WIDTH), ("gates", 3 * D_MODEL))
_MAIN_WIDTH = sum(w for _, w in _MAIN_OUTS)


def _params(*sem):
    return pltpu.CompilerParams(dimension_semantics=sem, vmem_limit_bytes=VMEM_LIMIT)


def _softplus(z):
    return jnp.maximum(z, 0.0) + jnp.log1p(jnp.exp(-jnp.abs(z)))


def _sigmoid(z):
    return 1.0 / (1.0 + jnp.exp(-z))


def _rms(x):
    return x * lax.rsqrt(jnp.mean(x * x, axis=-1, keepdims=True) + EPS)


def _modulate(x, g, shift, scale):
    return (_rms(x) * g) * (1.0 + scale) + shift


def _mod_spec(mod, tm, tiles_per_row):
    return pl.BlockSpec((None, mod.shape[1], mod.shape[2]), lambda i: (i // tiles_per_row, 0, 0))


def _ada_kernel(c_ref, w_ref, b_ref, o_ref):
    c = c_ref[...]
    a = c * _sigmoid(c)
    o_ref[...] = jnp.dot(a, w_ref[...], preferred_element_type=F32, precision=lax.Precision.HIGHEST) + b_ref[...]


def adaln_all(c_all, w_ada, b_ada):
    depth, d, d6 = w_ada.shape
    r = c_all.shape[0]
    tn = 1024
    return pl.pallas_call(
        _ada_kernel,
        grid=(depth, d6 // tn),
        in_specs=[pl.BlockSpec((r, d), lambda l, j: (0, 0)),
                  pl.BlockSpec((None, d, tn), lambda l, j: (l, 0, j)),
                  pl.BlockSpec((None, 1, tn), lambda l, j: (l, 0, j))],
        out_specs=pl.BlockSpec((None, r, tn), lambda l, j: (l, 0, j)),
        out_shape=jax.ShapeDtypeStruct((depth, r, d6), F32),
        compiler_params=_params("arbitrary", "arbitrary"),
        name="adaln",
    )(c_all, w_ada, b_ada.reshape(depth, 1, d6))


def _inproj_kernel(x_ref, g_ref, sh_ref, sc_ref, wm_ref, wif_ref, *out_refs):
    h = _modulate(x_ref[...], g_ref[...], sh_ref[...], sc_ref[...])
    hb = h.astype(BF16)
    off = 0
    for (_, width), o_ref in zip(_MAIN_OUTS, out_refs[:-1]):
        o_ref[...] = jnp.dot(hb, wm_ref[:, off:off + width], preferred_element_type=F32)
        off += width
    out_refs[-1][...] = jnp.dot(h, wif_ref[...], preferred_element_type=F32, precision=lax.Precision.HIGHEST)


def prep_inproj_weights(w_in_l):
    w_main = jnp.concatenate([w_in_l[:, :_OFF_ML_I], w_in_l[:, _OFF_GATES:]], axis=1).astype(BF16)
    w_if = jnp.pad(w_in_l[:, _OFF_ML_I:_OFF_GATES], ((0, 0), (0, LANES - 2 * ML_HEADS)))
    return w_main, w_if


def inproj(x, g, shift, scale, w_main, w_if, *, tm, tiles_per_row):
    n, d = x.shape
    row = lambda width: pl.BlockSpec((tm, width), lambda i: (i, 0))
    const = lambda shape: pl.BlockSpec(shape, lambda i: (0,) * len(shape), pipeline_mode=pl.Buffered(1))
    outs = pl.pallas_call(
        _inproj_kernel,
        grid=(n // tm,),
        in_specs=[row(d), const((1, d)), _mod_spec(shift, tm, tiles_per_row), _mod_spec(scale, tm, tiles_per_row),
                  const(w_main.shape), const(w_if.shape)],
        out_specs=[row(w) for _, w in _MAIN_OUTS] + [row(LANES)],
        out_shape=[jax.ShapeDtypeStruct((n, w), F32) for _, w in _MAIN_OUTS] + [jax.ShapeDtypeStruct((n, LANES), F32)],
        compiler_params=_params("arbitrary"),
        name="inproj",
    )(x, g.reshape(1, d), shift, scale, w_main, w_if)
    res = {name: o for (name, _), o in zip(_MAIN_OUTS, outs[:-1])}
    res["ml_if"] = outs[-1]
    return res


def _merge_kernel(x_ref, ga_ref, sb_ref, df_ref, mlh_ref, mlo_ref, gt_ref, gdf_ref, gml_ref,
                  wsb_ref, wdf_ref, wml_ref, wout_ref, o_ref, *, lam_init):
    d = x_ref.shape[-1]
    sb = jnp.dot(sb_ref[...].astype(BF16), wsb_ref[...], preferred_element_type=F32)
    gdf = gdf_ref[...] * (1.0 - lam_init)
    dfn = [(_rms(df_ref[:, h * DF_VDIM:(h + 1) * DF_VDIM]) * gdf).astype(BF16) for h in range(DF_HEADS)]
    df = jnp.dot(jnp.concatenate(dfn, axis=1), wdf_ref[...], preferred_element_type=F32)
    mln = []
    for h in range(ML_HEADS):
        s = slice(h * ML_DH, (h + 1) * ML_DH)
        mln.append((_rms(mlh_ref[:, s]) * gml_ref[:, s] * _sigmoid(mlo_ref[:, s])).astype(BF16))
    ml = jnp.dot(jnp.concatenate(mln, axis=1), wml_ref[...], preferred_element_type=F32)
    merged = (_sigmoid(gt_ref[:, 0:d]) * sb + _sigmoid(gt_ref[:, d:2 * d]) * df
              + _sigmoid(gt_ref[:, 2 * d:3 * d]) * ml)
    y = jnp.dot(merged.astype(BF16), wout_ref[...], preferred_element_type=F32)
    o_ref[...] = x_ref[...] + ga_ref[...] * y


def merge(x, ga, sb_o, df_o, ml_h, ml_o, gates, g_diff, g_ml, wsb, wdf, wml, wout, *, lam_init, tm, tiles_per_row):
    n, d = x.shape
    row = lambda width: pl.BlockSpec((tm, width), lambda i: (i, 0))
    const = lambda shape: pl.BlockSpec(shape, lambda i: (0,) * len(shape))
    return pl.pallas_call(
        functools.partial(_merge_kernel, lam_init=lam_init),
        grid=(n // tm,),
        in_specs=[row(d), _mod_spec(ga, tm, tiles_per_row), row(SB_WIDTH), row(DF_WIDTH), row(ML_WIDTH), row(ML_WIDTH),
                  row(3 * d), const((1, DF_VDIM)), const((1, ML_WIDTH)),
                  const(wsb.shape), const(wdf.shape), const(wml.shape), const(wout.shape)],
        out_specs=row(d),
        out_shape=jax.ShapeDtypeStruct((n, d), F32),
        compiler_params=_params("arbitrary"),
        name="merge",
    )(x, ga, sb_o, df_o, ml_h, ml_o, gates, g_diff.reshape(1, DF_VDIM), g_ml.reshape(1, ML_WIDTH), wsb, wdf, wml, wout)


def _final_norm_kernel(x_ref, g_ref, o_ref):
    o_ref[...] = _rms(x_ref[...]) * g_ref[...]


def final_norm(x, g, *, tm):
    n, d = x.shape
    return pl.pallas_call(
        _final_norm_kernel,
        grid=(n // tm,),
        in_specs=[pl.BlockSpec((tm, d), lambda i: (i, 0)), pl.BlockSpec((1, d), lambda i: (0, 0))],
        out_specs=pl.BlockSpec((tm, d), lambda i: (i, 0)),
        out_shape=jax.ShapeDtypeStruct((n, d), F32),
        compiler_params=_params("arbitrary"),
        name="final_norm",
    )(x, g.reshape(1, d))


def _split_bf16(x):
    hi = x.astype(BF16)
    return hi, (x - hi.astype(F32)).astype(BF16)


def _sb_kernel(q_ref, k_ref, v_ref, o_ref, *, tq, tk):
    qi = pl.program_id(2)
    lane = lax.broadcasted_iota(jnp.int32, (tq, LANES), 1)
    q = q_ref[...] * (SB_DH ** -0.5)
    diff = lax.broadcasted_iota(jnp.int32, (tq, tk), 1) - lax.broadcasted_iota(jnp.int32, (tq, tk), 0)
    sr = lax.broadcasted_iota(jnp.int32, (tk, tk), 0)
    sc = lax.broadcasted_iota(jnp.int32, (tk, tk), 1)
    suffix = (sr > sc).astype(BF16)
    nt = (((1,), (1,)), ((), ()))
    per_slab = LANES // SB_DH
    n_slabs = q_ref.shape[1] // LANES
    n_heads = per_slab * n_slabs
    q_heads = [(sl, jnp.where(lane // SB_DH == h, q[:, sl * LANES:(sl + 1) * LANES], 0.0).astype(BF16))
               for sl in range(n_slabs) for h in range(per_slab)]

    def tile(j, carry, masked):
        start = pl.multiple_of(j * tk, tk)
        kts = [k_ref[pl.ds(start, tk), sl * LANES:(sl + 1) * LANES].astype(BF16) for sl in range(n_slabs)]
        vts = [v_ref[pl.ds(start, tk), sl * LANES:(sl + 1) * LANES].astype(BF16) for sl in range(n_slabs)]
        earlier = diff < qi * tq - j * tk
        out = []
        for (sl, qh), (acc, run) in zip(q_heads, carry):
            kt, vt = kts[sl], vts[sl]
            z2 = lax.dot_general(qh, kt, nt, preferred_element_type=F32) * LOG2E
            nz2 = -z2
            soft = jnp.log2(1.0 + jnp.exp2(jnp.minimum(z2, nz2)))
            log_keep = jnp.minimum(nz2, 0.0) - soft
            if masked:
                log_keep = jnp.where(earlier, log_keep, 0.0)
            within = jnp.dot(log_keep.astype(BF16), suffix, preferred_element_type=F32)
            w = jnp.exp2((z2 + log_keep) + within + run)
            if masked:
                w = jnp.where(earlier, w, 0.0)
            out.append((acc + jnp.dot(w.astype(BF16), vt, preferred_element_type=F32),
                        run + jnp.sum(log_keep, axis=-1, keepdims=True)))
        return tuple(out)

    carry = ((jnp.zeros((tq, LANES), F32), jnp.zeros((tq, 1), F32)),) * n_heads
    first = (qi * tq) // tk
    for r in reversed(range(tq // tk)):
        carry = tile(first + r, carry, True)
    carry = lax.fori_loop(0, first, lambda i, c: tile(first - 1 - i, c, False), carry)
    for sl in range(n_slabs):
        o_ref[:, sl * LANES:(sl + 1) * LANES] = jnp.where(lane < SB_DH, carry[per_slab * sl][0], carry[per_slab * sl + 1][0])


def sb_attention(q, k, v, *, batch, tq, tk, slabs=2):
    n, width = q.shape
    t = n // batch
    nq = t // tq
    kv = lambda a: a.reshape(batch, t, width)
    kv_spec = pl.BlockSpec((None, t, slabs * LANES), lambda b, hp, i: (b, 0, hp))
    q_spec = pl.BlockSpec((tq, slabs * LANES), lambda b, hp, i: (b * nq + i, hp))
    return pl.pallas_call(
        functools.partial(_sb_kernel, tq=tq, tk=tk),
        grid=(batch, width // (slabs * LANES), nq),
        in_specs=[q_spec, kv_spec, kv_spec],
        out_specs=q_spec,
        out_shape=jax.ShapeDtypeStruct((n, width), F32),
        compiler_params=_params("arbitrary", "arbitrary", "arbitrary"),
        name="sb_attention",
    )(q, kv(k), kv(v))


def _lambda_value(lam_ref, lam_init):
    lv = lam_ref[...]
    s1 = jnp.sum(lv[0:1, :] * lv[1:2, :], axis=-1, keepdims=True)
    s2 = jnp.sum(lv[2:3, :] * lv[3:4, :], axis=-1, keepdims=True)
    return jnp.exp(s1) - jnp.exp(s2) + lam_init


def _df_kernel(slope_ref, q_ref, k_ref, v_ref, lam_ref, o_ref, *, tq, tk, lam_init):
    group = DF_HEADS // DF_KV_HEADS
    kvh = pl.program_id(1)
    qi = pl.program_id(2)
    lane = lax.broadcasted_iota(jnp.int32, (tq, LANES), 1)
    row = lax.broadcasted_iota(jnp.int32, (tq, tk), 0)
    col = lax.broadcasted_iota(jnp.int32, (tq, tk), 1)
    diff = (col - row).astype(F32)
    ones = jnp.ones((tk, LANES), BF16)
    nt = (((1,), (1,)), ((), ()))
    chains = []
    for g in range(group):
        slope = slope_ref[kvh * group + g]
        qg = q_ref[:, g * LANES:(g + 1) * LANES] * (DF_DH ** -0.5)
        rel = diff * slope
        for m in range(2):
            chains.append((slope, rel, jnp.where(lane // DF_DH == m, qg, 0.0).astype(BF16)))

    def tile(j, carry, masked):
        start = pl.multiple_of(j * tk, tk)
        kt = k_ref[pl.ds(start, tk), :].astype(BF16)
        vt = jnp.concatenate([v_ref[pl.ds(start, tk), :].astype(BF16), ones], axis=1)
        shift = (qi * tq - j * tk).astype(F32)
        out = []
        for (slope, rel, qm), (acc, mx) in zip(chains, carry):
            s = lax.dot_general(qm, kt, nt, preferred_element_type=F32) + rel
            if masked:
                s = jnp.where(diff <= shift, s, NEG)
            off = -slope * shift
            mx_new = jnp.maximum(mx, jnp.max(s, axis=-1, keepdims=True) + off)
            e = jnp.exp(s + (off - mx_new))
            out.append((jnp.exp(mx - mx_new) * acc + jnp.dot(e.astype(BF16), vt, preferred_element_type=F32), mx_new))
        return tuple(out)

    init = tuple((jnp.zeros((tq, 2 * LANES), F32), jnp.full((tq, 1), NEG, F32)) for _ in chains)
    n_full = (qi * tq) // tk
    carry = lax.fori_loop(0, n_full, lambda j, c: tile(j, c, False), init)
    carry = tile(n_full, carry, True)
    lam = _lambda_value(lam_ref, lam_init)
    for g in range(group):
        res = [carry[2 * g + m][0][:, :LANES] / carry[2 * g + m][0][:, LANES:] for m in range(2)]
        o_ref[:, g * LANES:(g + 1) * LANES] = res[0] - lam * res[1]


def alibi_slope_table():
    return jnp.array([2.0 ** (-8.0 * (h + 1) / DF_HEADS) for h in range(DF_HEADS)], dtype=F32)


def df_attention(q, k, v, lam_vecs, *, lam_init, batch, tq, tk):
    n, width = q.shape
    t = n // batch
    nq = t // tq
    kv = lambda a: a.reshape(batch, t, a.shape[-1])
    group = DF_HEADS // DF_KV_HEADS
    kv_spec = pl.BlockSpec((None, t, LANES), lambda b, h, i, s: (b, 0, h))
    q_spec = pl.BlockSpec((tq, group * LANES), lambda b, h, i, s: (b * nq + i, h))
    return pl.pallas_call(
        functools.partial(_df_kernel, tq=tq, tk=tk, lam_init=lam_init),
        grid_spec=pltpu.PrefetchScalarGridSpec(
            num_scalar_prefetch=1,
            grid=(batch, DF_KV_HEADS, nq),
            in_specs=[q_spec, kv_spec, kv_spec, pl.BlockSpec(lam_vecs.shape, lambda b, h, i, s: (0, 0))],
            out_specs=q_spec),
        out_shape=jax.ShapeDtypeStruct((n, width), F32),
        compiler_params=_params("arbitrary", "arbitrary", "arbitrary"),
        name="df_attention",
    )(alibi_slope_table(), q, kv(k), kv(v), lam_vecs)


def _mlstm_kernel(qk_ref, v_ref, if_ref, conv0_ref, c0_ref, n0_ref, m0_ref, wc_ref, bc_ref, bif_ref,
                  h_ref, c_ref, n_ref, m_ref, buf, cs, ns, ms, *, valid):
    L = ML_CHUNK
    ci = pl.program_id(1)

    @pl.when(ci == 0)
    def _():
        buf[0:SUBLANES, :] = conv0_ref[...]
        cs[...] = c0_ref[...]
        ns[...] = n0_ref[...]
        ms[...] = m0_ref[...]

    cur = qk_ref[...]
    buf[SUBLANES:SUBLANES + L, :] = cur
    acc = bc_ref[...]
    for j in range(ML_CONV):
        lo = SUBLANES - (ML_CONV - 1) + j
        acc = acc + buf[lo:lo + L, :] * wc_ref[j:j + 1, :]
    qk = acc * _sigmoid(acc)
    buf[0:SUBLANES, :] = cur[L - SUBLANES:L, :]

    row = lax.broadcasted_iota(jnp.int32, (L, LANES), 0)
    lane = lax.broadcasted_iota(jnp.int32, (L, LANES), 1)
    gt = if_ref[...] + bif_ref[...]
    is_valid = row < valid
    ig = jnp.where(is_valid, gt, NEG)
    lf = jnp.where(is_valid & (lane >= ML_HEADS) & (lane < 2 * ML_HEADS), -_softplus(-gt), 0.0)
    r2 = lax.broadcasted_iota(jnp.int32, (L, L), 0)
    c2 = lax.broadcasted_iota(jnp.int32, (L, L), 1)
    causal = c2 <= r2
    bcum = jnp.dot(causal.astype(F32), lf, preferred_element_type=F32, precision=lax.Precision.HIGHEST)
    ig_t = ig.T
    bcum_t = bcum.T
    nt = (((1,), (1,)), ((), ()))
    for h in range(ML_HEADS):
        b_col = bcum[:, ML_HEADS + h:ML_HEADS + h + 1]
        b_row = bcum_t[ML_HEADS + h:ML_HEADS + h + 1, :]
        ig_row = ig_t[h:h + 1, :]
        ig_col = ig[:, h:h + 1]
        m_run = ms[h:h + 1, 0:1]
        log_d = jnp.where(causal, b_col - b_row + ig_row, NEG)
        log_inter = b_col + m_run
        m_t = jnp.maximum(log_inter, jnp.max(log_d, axis=-1, keepdims=True))
        a_inter = jnp.exp(log_inter - m_t)
        dmat = jnp.exp(log_d - m_t)
        hs = slice(h * ML_DH, (h + 1) * ML_DH)
        qh = qk[:, hs]
        kh = qk[:, ML_WIDTH + h * ML_DH:ML_WIDTH + (h + 1) * ML_DH] * (ML_DH ** -0.5)
        qb, kb, vb = qh.astype(BF16), kh.astype(BF16), v_ref[:, hs].astype(BF16)
        scores = lax.dot_general(qb, kb, nt, preferred_element_type=F32) * dmat
        c_mat = cs[h]
        n_vec = ns[h:h + 1, :]
        num = (a_inter * jnp.dot(qb, c_mat.astype(BF16), preferred_element_type=F32)
               + jnp.dot(scores.astype(BF16), vb, preferred_element_type=F32))
        den = a_inter * jnp.sum(qh * n_vec, axis=-1, keepdims=True) + jnp.sum(scores, axis=-1, keepdims=True)
        h_ref[:, hs] = num / jnp.maximum(jnp.abs(den), jnp.exp(-m_t))
        m_new = m_t[L - 1:L, :]
        b_last = b_col[L - 1:L, :]
        w_end = jnp.exp(b_last - b_col + ig_col - m_new)
        decay = jnp.exp(b_last + m_run - m_new)
        kw = kh * w_end
        cs[h] = decay * c_mat + jnp.dot(kw.T.astype(BF16), vb, preferred_element_type=F32)
        ns[h:h + 1, :] = decay * n_vec + jnp.sum(kw, axis=0, keepdims=True)
        ms[h:h + 1, :] = jnp.broadcast_to(m_new, (1, LANES))

    @pl.when(ci == pl.num_programs(1) - 1)
    def _():
        c_ref[...] = cs[...]
        n_ref[...] = ns[...]
        m_ref[...] = ms[...]


def mlstm(ml_qk, ml_v, ml_if, conv8, c0, n0, m0, w_conv, b_conv, b_ig, b_fg, *, batch, valid):
    n_tok = ml_qk.shape[0]
    L = ML_CHUNK
    nc = n_tok // batch // L
    pad_rows = lambda a: jnp.pad(a, ((0, 0), (0, SUBLANES - ML_HEADS), (0, 0)))
    n0p = pad_rows(n0)
    m0p = pad_rows(jnp.broadcast_to(m0[:, :, None], (batch, ML_HEADS, LANES)))
    bias_if = jnp.pad(jnp.concatenate([b_ig, b_fg]), (0, LANES - 2 * ML_HEADS)).reshape(1, LANES)
    tok = lambda width: pl.BlockSpec((L, width), lambda b, c: (b * nc + c, 0))
    per_b = lambda *shape: pl.BlockSpec((None,) + shape, lambda b, c: (b,) + (0,) * len(shape))
    const = lambda *shape: pl.BlockSpec(shape, lambda b, c: (0,) * len(shape))
    h, c_out, n_out, m_out = pl.pallas_call(
        functools.partial(_mlstm_kernel, valid=valid),
        grid=(batch, nc),
        in_specs=[tok(2 * ML_WIDTH), tok(ML_WIDTH), tok(LANES), per_b(SUBLANES, 2 * ML_WIDTH),
                  per_b(ML_HEADS, ML_DH, ML_DH), per_b(SUBLANES, LANES), per_b(SUBLANES, LANES),
                  const(ML_CONV, 2 * ML_WIDTH), const(1, 2 * ML_WIDTH), const(1, LANES)],
        out_specs=[tok(ML_WIDTH), per_b(ML_HEADS, ML_DH, ML_DH), per_b(SUBLANES, LANES), per_b(SUBLANES, LANES)],
        out_shape=[jax.ShapeDtypeStruct((n_tok, ML_WIDTH), F32), jax.ShapeDtypeStruct((batch, ML_HEADS, ML_DH, ML_DH), F32),
                   jax.ShapeDtypeStruct((batch, SUBLANES, LANES), F32), jax.ShapeDtypeStruct((batch, SUBLANES, LANES), F32)],
        scratch_shapes=[pltpu.VMEM((SUBLANES + L, 2 * ML_WIDTH), F32), pltpu.VMEM((ML_HEADS, ML_DH, ML_DH), F32),
                        pltpu.VMEM((SUBLANES, LANES), F32), pltpu.VMEM((SUBLANES, LANES), F32)],
        compiler_params=_params("arbitrary", "arbitrary"),
        name="mlstm",
    )(ml_qk, ml_v, ml_if, conv8, c0, n0p, m0p, w_conv, b_conv.reshape(1, -1), bias_if)
    return h, c_out, n_out[:, :ML_HEADS], m_out[:, :ML_HEADS, 0]


def _top16(s):
    rows = lax.broadcasted_iota(jnp.int32, (PK_TOPK, s.shape[1]), 0)
    work = s
    rank = jnp.full(s.shape, float(PK_TOPK), F32)
    top = jnp.zeros((PK_TOPK, s.shape[1]), F32)
    for k in range(PK_TOPK):
        m = jnp.max(work, axis=0, keepdims=True)
        sel = work == m
        rank = jnp.where(sel, float(k), rank)
        work = jnp.where(sel, NEG, work)
        top = jnp.where(rows == k, m, top)
    return top, rank


def _route_kernel(x_ref, g_ref, sh_ref, sc_ref, wpq_ref, keys_ref, xt_ref, r2_ref, nb_ref, e1_ref, e2_ref):
    tm = x_ref.shape[0]
    x2 = _modulate(x_ref[...], g_ref[...], sh_ref[...], sc_ref[...])
    xt_ref[...] = x2.T.astype(BF16)
    q = jnp.dot(x2.astype(BF16), wpq_ref[...], preferred_element_type=F32)
    nt = (((1,), (1,)), ((), ()))
    row8 = lax.broadcasted_iota(jnp.int32, (SUBLANES, tm), 0)
    for h in range(PK_HEADS):
        s, top, rank = [], [], []
        for p in range(2):
            lo = (2 * h + p) * PK_HALF
            sp = lax.dot_general(keys_ref[h, p], q[:, lo:lo + PK_HALF].astype(BF16), nt, preferred_element_type=F32)
            tp, rk = _top16(sp)
            s.append(sp)
            top.append(tp)
            rank.append(rk)
        blocks = [top[0][0:1, :] + top[1]]
        for a in range(1, PK_TOPK):
            width = PK_TOPK // (a + 1)
            blocks.append(jnp.where(row8 < width, top[0][a:a + 1, :] + top[1][0:SUBLANES, :], NEG))
        cand = jnp.concatenate(blocks, axis=0)
        work = cand
        for k in range(PK_TOPK):
            thr = jnp.max(work, axis=0, keepdims=True)
            work = jnp.where(work == thr, NEG, work)
        chosen = cand >= thr
        cmax = top[0][0:1, :] + top[1][0:1, :]
        z = jnp.sum(jnp.where(chosen, jnp.exp(cand - cmax), 0.0), axis=0, keepdims=True)
        chosen_f = chosen.astype(F32)
        nb_rows = jnp.zeros_like(rank[0])
        lo = 0
        for a in range(PK_TOPK):
            size = PK_TOPK if a == 0 else SUBLANES
            nb_a = jnp.sum(chosen_f[lo:lo + size, :], axis=0, keepdims=True)
            nb_rows = jnp.where(rank[0] == float(a), nb_a, nb_rows)
            lo += size
        r2_ref[h] = rank[1].astype(BF16)
        nb_ref[h] = nb_rows
        e1_ref[h] = jnp.exp(s[0] - top[0][0:1, :]) / z
        e2_ref[h] = jnp.exp(s[1] - top[1][0:1, :]).astype(BF16)


def peer_route(x, g, shift, scale, w_pq, sub_keys, *, tm, tiles_per_row):
    n, d = x.shape
    const = lambda shape: pl.BlockSpec(shape, lambda i: (0,) * len(shape))
    lanes = pl.BlockSpec((PK_HEADS, PK_NKEYS, tm), lambda i: (0, 0, i))
    routed = lambda dtype: jax.ShapeDtypeStruct((PK_HEADS, PK_NKEYS, n), dtype)
    return pl.pallas_call(
        _route_kernel,
        grid=(n // tm,),
        in_specs=[pl.BlockSpec((tm, d), lambda i: (i, 0)), const((1, d)), _mod_spec(shift, tm, tiles_per_row),
                  _mod_spec(scale, tm, tiles_per_row), const(w_pq.shape), const(sub_keys.shape)],
        out_specs=[pl.BlockSpec((d, tm), lambda i: (0, i)), lanes, lanes, lanes, lanes],
        out_shape=[jax.ShapeDtypeStruct((d, n), BF16), routed(BF16), routed(F32), routed(F32), routed(BF16)],
        compiler_params=_params("arbitrary"),
        name="peer_route",
    )(x, g.reshape(1, d), shift, scale, w_pq, sub_keys)


def _gelu_tanh(x):
    return 0.5 * x * (1.0 + jnp.tanh(math.sqrt(2.0 / math.pi) * (x + 0.044715 * (x * x * x))))


BF16_ROWS = 16


def _expert_kernel(x_ref, ga_ref, xt_ref, r2_ref, nb_ref, e1_ref, e2_ref, u_ref, vt_ref, o_ref, acc, *, sub):
    c = pl.program_id(1)
    tm = xt_ref.shape[1]
    groups = PK_NKEYS // BF16_ROWS

    @pl.when(c == 0)
    def _():
        acc[...] = jnp.zeros_like(acc)

    xt = xt_ref[...]
    rank2 = [r2_ref[h].reshape(groups, BF16_ROWS, tm) for h in range(PK_HEADS)]
    gate2 = [e2_ref[h].reshape(groups, BF16_ROWS, tm) for h in range(PK_HEADS)]
    tile_row = lambda ref, h, e: jnp.broadcast_to(ref[h, e:e + 1, :], (BF16_ROWS, tm)).astype(BF16)[None]
    for s0 in range(0, u_ref.shape[0], sub):
        act = _gelu_tanh(jnp.dot(u_ref[s0:s0 + sub, :], xt, preferred_element_type=F32).astype(BF16))
        slabs = []
        for e in range(s0 // PK_NKEYS, (s0 + sub) // PK_NKEYS):
            gate = None
            for h in range(PK_HEADS):
                term = jnp.where(rank2[h] < tile_row(nb_ref, h, e), gate2[h], jnp.zeros_like(gate2[h])) * tile_row(e1_ref, h, e)
                gate = term if gate is None else gate + term
            slabs.append(gate.reshape(PK_NKEYS, tm))
        weighted = jnp.concatenate(slabs, axis=0) * act
        acc[...] += jnp.dot(vt_ref[:, s0:s0 + sub], weighted, preferred_element_type=F32)

    @pl.when(c == pl.num_programs(1) - 1)
    def _():
        o_ref[...] = x_ref[...] + ga_ref[...] * acc[...].T


def peer_expert(x, ga, xt, r2, nb, e1c, e2, u, vt, *, tm, tiles_per_row, chunk=2048, sub=2048):
    n, d = x.shape
    n_exp = u.shape[0]
    e1_per_chunk = chunk // PK_NKEYS
    lanes = pl.BlockSpec((PK_HEADS, PK_NKEYS, tm), lambda i, c: (0, 0, i))
    by_e1 = pl.BlockSpec((PK_HEADS, e1_per_chunk, tm), lambda i, c: (0, c, i))
    mod = pl.BlockSpec((None, ga.shape[1], ga.shape[2]), lambda i, c: (i // tiles_per_row, 0, 0))
    return pl.pallas_call(
        functools.partial(_expert_kernel, sub=sub),
        grid=(n // tm, n_exp // chunk),
        in_specs=[pl.BlockSpec((tm, d), lambda i, c: (i, 0)), mod, pl.BlockSpec((d, tm), lambda i, c: (0, i)),
                  lanes, by_e1, by_e1, lanes,
                  pl.BlockSpec((chunk, d), lambda i, c: (c, 0)), pl.BlockSpec((d, chunk), lambda i, c: (0, c))],
        out_specs=pl.BlockSpec((tm, d), lambda i, c: (i, 0)),
        out_shape=jax.ShapeDtypeStruct((n, d), F32),
        scratch_shapes=[pltpu.VMEM((d, tm), F32)],
        compiler_params=_params("arbitrary", "arbitrary"),
        name="peer_expert",
    )(x, ga, xt, r2, nb, e1c, e2, u, vt)


PAGES_PER_STEP = 8


def _page_specs(n_pages, rows, width, layer, pps):
    return [pl.BlockSpec((None, None, rows, width),
                         lambda b, s, pt, k=k: (layer, pt[b, n_pages - 1 - (s * pps + k)], 0, 0)) for k in range(pps)]


def _sbs_kernel(pt_ref, qb_ref, kn_ref, vn_ref, sfx_ref, *refs, pps):
    k_refs, v_refs, o_ref, acc, run = refs[:pps], refs[pps:2 * pps], refs[2 * pps], refs[2 * pps + 1], refs[2 * pps + 2]
    s = pl.program_id(1)
    n_q, page = run.shape
    t_new = o_ref.shape[0]
    qb = (qb_ref[...] * (SB_DH ** -0.5)).astype(BF16)
    nt = (((1,), (1,)), ((), ()))

    def process(pages, mask):
        zs = [jnp.dot(qb, k_ref[...].astype(BF16), preferred_element_type=F32) for k_ref, _ in pages]
        sps = [_softplus(z) for z in zs]
        log_keeps = [-sp if mask is None else jnp.where(mask, -sp, 0.0) for sp in sps]
        hi, lo = _split_bf16(jnp.concatenate(log_keeps, axis=0))
        suffix = sfx_ref[...]
        within = jnp.dot(hi, suffix, preferred_element_type=F32) + jnp.dot(lo, suffix, preferred_element_type=F32)
        running = run[:, 0:1]
        out = acc[...]
        for i, (z, sp, log_keep, (_, v_ref)) in enumerate(zip(zs, sps, log_keeps, pages)):
            w = jnp.exp((z - sp) + within[i * n_q:(i + 1) * n_q, :] + running)
            if mask is not None:
                w = jnp.where(mask, w, 0.0)
            out = out + lax.dot_general(w.astype(BF16), v_ref[...].astype(BF16), nt, preferred_element_type=F32)
            running = running + jnp.sum(log_keep, axis=1, keepdims=True)
        acc[...] = out
        run[...] = jnp.broadcast_to(running, run.shape)

    @pl.when(s == 0)
    def _():
        acc[...] = jnp.zeros_like(acc)
        run[...] = jnp.zeros_like(run)
        pos = lax.broadcasted_iota(jnp.int32, (n_q, page), 1)
        t_row = lax.broadcasted_iota(jnp.int32, (n_q, page), 0) // SB_HEADS
        process([(kn_ref, vn_ref)], pos < t_row)

    process(list(zip(k_refs, v_refs)), None)

    @pl.when(s == pl.num_programs(1) - 1)
    def _():
        h_row = lax.broadcasted_iota(jnp.int32, (SB_HEADS, SB_WIDTH), 0)
        h_col = lax.broadcasted_iota(jnp.int32, (SB_HEADS, SB_WIDTH), 1) // SB_DH
        own = acc[...].reshape(t_new, SB_HEADS, SB_WIDTH) * (h_row == h_col).astype(F32)
        o_ref[...] = jnp.sum(own, axis=1)


def sb_attention_sample(page_table, q, k_new, v_new, cache_k, cache_v, *, layer, pps=PAGES_PER_STEP):
    nb, t_new, width = q.shape
    n_pages = page_table.shape[1]
    pps = math.gcd(pps, n_pages)
    depth, n_pool, page, heads, dh = cache_k.shape
    n_q = t_new * heads
    eye = jnp.eye(heads, dtype=F32)[None, None, :, :, None]
    qb = (q.reshape(nb, t_new, heads, 1, dh) * eye).reshape(nb, n_q, width)
    new_page = lambda a: jnp.pad(a.transpose(0, 2, 1), ((0, 0), (0, 0), (0, page - t_new)))
    paged = lambda a: a.transpose(0, 1, 3, 4, 2).reshape(depth, n_pool, width, page)
    per_b = lambda *shape: pl.BlockSpec((None,) + shape, lambda b, s, pt: (b,) + (0,) * len(shape))
    key = jnp.arange(page)
    suffix = (key[:, None] > key[None, :]).astype(BF16)
    return pl.pallas_call(
        functools.partial(_sbs_kernel, pps=pps),
        grid_spec=pltpu.PrefetchScalarGridSpec(
            num_scalar_prefetch=1,
            grid=(nb, n_pages // pps),
            in_specs=[per_b(n_q, width), per_b(width, page), per_b(width, page),
                      pl.BlockSpec(suffix.shape, lambda b, s, pt: (0, 0))]
            + _page_specs(n_pages, width, page, layer, pps) + _page_specs(n_pages, width, page, layer, pps),
            out_specs=per_b(t_new, width),
            scratch_shapes=[pltpu.VMEM((n_q, width), F32), pltpu.VMEM((n_q, page), F32)]),
        out_shape=jax.ShapeDtypeStruct((nb, t_new, width), F32),
        compiler_params=_params("arbitrary", "arbitrary"),
        name="sb_attention_sample",
    )(page_table, qb, new_page(k_new), new_page(v_new), suffix, *([paged(cache_k)] * pps), *([paged(cache_v)] * pps))


def _dfs_kernel(pt_ref, qc_ref, kn_ref, vn_ref, col_ref, lam_ref, *refs, pps, n_pages, lam_init):
    k_refs, v_refs, o_ref, acc, mx = refs[:pps], refs[pps:2 * pps], refs[2 * pps], refs[2 * pps + 1], refs[2 * pps + 2]
    s = pl.program_id(1)
    rows = kn_ref.shape[0]
    page = rows // DF_KV_HEADS
    half = LANES // 4
    r = lax.broadcasted_iota(jnp.int32, (rows, LANES), 0)
    c = lax.broadcasted_iota(jnp.int32, (rows, LANES), 1)
    group = DF_HEADS // DF_KV_HEADS
    same_kv = (r % DF_KV_HEADS == (c // group) % DF_KV_HEADS) & (c < 2 * half)
    slope = col_ref[0:1, :]
    t_col = col_ref[1:2, :]
    tok = (r // DF_KV_HEADS).astype(F32)
    slope_tok = slope * tok
    qc = (qc_ref[...] * (DF_DH ** -0.5)).astype(BF16)

    def process(pages, mask):
        scores = []
        for k_ref, _, base in pages:
            z = jnp.dot(k_ref[...].astype(BF16), qc, preferred_element_type=F32)
            scores.append(jnp.where(mask, z + slope_tok + slope * (base - t_col), NEG))
        mx_old = mx[0:1, :]
        mx_new = mx_old
        for sb in scores:
            mx_new = jnp.maximum(mx_new, jnp.max(sb, axis=0, keepdims=True))
        e = jnp.concatenate([jnp.exp(sb - mx_new).astype(BF16) for sb in scores], axis=0)
        vt = jnp.concatenate([v_ref[...].T.astype(BF16) for _, v_ref, _ in pages], axis=1)
        vext = jnp.concatenate([vt, jnp.ones_like(vt)], axis=0)
        acc[...] = jnp.exp(mx_old - mx_new) * acc[...] + jnp.dot(vext, e, preferred_element_type=F32)
        mx[...] = jnp.broadcast_to(mx_new, mx.shape)

    @pl.when(s == 0)
    def _():
        acc[...] = jnp.zeros_like(acc)
        mx[...] = jnp.full_like(mx, NEG)
        process([(kn_ref, vn_ref, 0.0)], same_kv & (tok <= t_col))

    first_page = n_pages - 1 - s * pps
    process([(k_refs[k], v_refs[k], ((first_page - k - n_pages) * page).astype(F32)) for k in range(pps)], same_kv)

    @pl.when(s == pl.num_programs(1) - 1)
    def _():
        res = acc[0:LANES, :] / acc[LANES:2 * LANES, :]
        o_ref[...] = res - _lambda_value(lam_ref, lam_init) * pltpu.roll(res, LANES - half, axis=1)


def df_attention_sample(page_table, q, k_new, v_new, cache_k, cache_v, lam_vecs, *, layer, lam_init, pps=PAGES_PER_STEP):
    nb, t_new, _ = q.shape
    n_pages = page_table.shape[1]
    pps = math.gcd(pps, n_pages)
    depth, n_pool, page, kvh, width = cache_k.shape
    rows = page * kvh
    group = DF_HEADS // DF_KV_HEADS
    half = LANES // 4
    assert t_new * kvh * group == half and width == LANES
    qt = q.reshape(nb, t_new * kvh * group, width).transpose(0, 2, 1)
    d_half = (jnp.arange(width) // DF_DH)[None, :, None]
    qc = jnp.concatenate([jnp.where(d_half == 0, qt, 0.0), jnp.where(d_half == 1, qt, 0.0),
                          jnp.zeros((nb, width, LANES - 2 * half), F32)], axis=2)
    col = jnp.arange(LANES)
    head_of_col = ((col // group) % kvh) * group + col % group
    slope_col = jnp.where(col < 2 * half, alibi_slope_table()[head_of_col], 0.0)
    t_of_col = ((col % half) // (kvh * group)).astype(F32)
    col_info = jnp.zeros((SUBLANES, LANES), F32).at[0].set(slope_col).at[1].set(t_of_col)
    pad_page = lambda a: jnp.pad(a, ((0, 0), (0, rows - a.shape[1]), (0, 0)))
    paged = lambda a: a.reshape(depth, n_pool, rows, width)
    per_b = lambda *shape: pl.BlockSpec((None,) + shape, lambda b, s, pt: (b,) + (0,) * len(shape))
    const = lambda *shape: pl.BlockSpec(shape, lambda b, s, pt: (0,) * len(shape))
    out = pl.pallas_call(
        functools.partial(_dfs_kernel, pps=pps, n_pages=n_pages, lam_init=lam_init),
        grid_spec=pltpu.PrefetchScalarGridSpec(
            num_scalar_prefetch=1,
            grid=(nb, n_pages // pps),
            in_specs=[per_b(width, LANES), per_b(rows, width), per_b(rows, width), const(SUBLANES, LANES), const(*lam_vecs.shape)]
            + _page_specs(n_pages, rows, width, layer, pps) + _page_specs(n_pages, rows, width, layer, pps),
            out_specs=per_b(width, LANES),
            scratch_shapes=[pltpu.VMEM((2 * LANES, LANES), F32), pltpu.VMEM((SUBLANES, LANES), F32)]),
        out_shape=jax.ShapeDtypeStruct((nb, width, LANES), F32),
        compiler_params=_params("arbitrary", "arbitrary"),
        name="df_attention_sample",
    )(page_table, qc, pad_page(k_new), pad_page(v_new), col_info, lam_vecs, *([paged(cache_k)] * pps), *([paged(cache_v)] * pps))
    return out[:, :, :half].transpose(0, 2, 1).reshape(nb, t_new, kvh * group * width)


TOKEN_TILE = 256
ATTN_TILE = 256


def _token_stage(x, mods, attend, lam_init, weights, *, tm, tiles_per_row):
    (g1, g2, g_diff_l, g_ml_l, w_main, w_if, wsb, wdf, wml, wout, wpq, keys, u, vt) = weights
    p = inproj(x, g1, mods[0], mods[1], w_main, w_if, tm=tm, tiles_per_row=tiles_per_row)
    sb_o, df_o, ml_h, state = attend(p)
    x = merge(x, mods[2], sb_o, df_o, ml_h, p["ml_o"], p["gates"], g_diff_l, g_ml_l, wsb, wdf, wml, wout,
              lam_init=lam_init, tm=tm, tiles_per_row=tiles_per_row)
    routed = peer_route(x, g2, mods[3], mods[4], wpq, keys, tm=tm, tiles_per_row=tiles_per_row)
    wide = 2 if tiles_per_row % 2 == 0 else 1
    x = peer_expert(x, mods[5], *routed, u, vt, tm=wide * tm, tiles_per_row=tiles_per_row // wide)
    return x, p, state


def kernel(x_prompt, x_sample, cache_sb_k, cache_sb_v, cache_df_k, cache_df_v, state_ml_c, state_ml_n, state_ml_m, state_ml_conv, page_table, c_prompt, c_sample, w_ada, b_ada, g_norm1, g_norm2, g_final, w_in, w_conv, b_conv, b_ig, b_fg, g_ml, lam_q1, lam_k1, lam_q2, lam_k2, g_diff, w_br_sb, w_br_df, w_br_ml, w_out, w_pq, sub_keys, peer_u, peer_v):
    bsz, seq, d = x_prompt.shape
    nb, ts, _ = x_sample.shape
    depth = w_ada.shape[0]
    n_p, n_s = bsz * seq, nb * ts
    xp = x_prompt.reshape(n_p, d)
    xs = x_sample.reshape(n_s, d)
    mod = adaln_all(jnp.concatenate([c_prompt, c_sample], axis=0), w_ada, b_ada).reshape(depth, bsz + nb, 6, d)
    new_prompt, new_sample = [], []
    for l in range(depth):
        lam_init = 0.8 - 0.6 * math.exp(-0.3 * l)
        mods_p = [mod[l, :bsz, i][:, None, :] for i in range(6)]
        mods_s = [jnp.repeat(mod[l, bsz:, i], ts, axis=0)[None] for i in range(6)]
        lam_vecs = jnp.stack([lam_q1[l], lam_k1[l], lam_q2[l], lam_k2[l]])
        bf = lambda a: a.astype(BF16)
        weights = (g_norm1[l], g_norm2[l], g_diff[l], g_ml[l], *prep_inproj_weights(w_in[l]), bf(w_br_sb[l]), bf(w_br_df[l]),
                   bf(w_br_ml[l]), bf(w_out[l]), bf(w_pq[l]), bf(sub_keys[l]), bf(peer_u[l]), bf(peer_v[l].T))
        conv_w = (w_conv[l], b_conv[l], b_ig[l], b_fg[l])

        def attend_prompt(p):
            sb_o = sb_attention(p["sb_q"], p["sb_k"], p["sb_v"], batch=bsz, tq=min(2 * ATTN_TILE, seq), tk=ATTN_TILE)
            df_o = df_attention(p["df_q"], p["df_k"], p["df_v"], lam_vecs, lam_init=lam_init, batch=bsz,
                                tq=min(2 * ATTN_TILE, seq), tk=min(2 * ATTN_TILE, seq))
            zeros = lambda *shape: jnp.zeros((bsz,) + shape, F32)
            ml_h, *state = mlstm(p["ml_qk"], p["ml_v"], p["ml_if"], zeros(SUBLANES, 2 * ML_WIDTH), zeros(ML_HEADS, ML_DH, ML_DH),
                                 zeros(ML_HEADS, ML_DH), zeros(ML_HEADS), *conv_w, batch=bsz, valid=ML_CHUNK)
            return sb_o, df_o, ml_h, state

        def attend_sample(p):
            per_seq = lambda a, rows: a.reshape(nb, rows, a.shape[-1] * ts // rows)
            sb_o = sb_attention_sample(page_table, per_seq(p["sb_q"], ts), per_seq(p["sb_k"], ts), per_seq(p["sb_v"], ts),
                                       cache_sb_k, cache_sb_v, layer=l)
            df_o = df_attention_sample(page_table, per_seq(p["df_q"], ts), per_seq(p["df_k"], ts * DF_KV_HEADS),
                                       per_seq(p["df_v"], ts * DF_KV_HEADS), cache_df_k, cache_df_v, lam_vecs,
                                       layer=l, lam_init=lam_init)
            chunked = lambda a: jnp.pad(per_seq(a, ts), ((0, 0), (0, ML_CHUNK - ts), (0, 0))).reshape(nb * ML_CHUNK, a.shape[-1])
            conv8 = jnp.pad(state_ml_conv[l], ((0, 0), (SUBLANES - (ML_CONV - 1), 0), (0, 0)))
            ml_h, *state = mlstm(chunked(p["ml_qk"]), chunked(p["ml_v"]), chunked(p["ml_if"]), conv8, state_ml_c[l], state_ml_n[l],
                                 state_ml_m[l], *conv_w, batch=nb, valid=ts)
            ml_h = ml_h.reshape(nb, ML_CHUNK, ML_WIDTH)[:, :ts].reshape(n_s, ML_WIDTH)
            return sb_o.reshape(n_s, SB_WIDTH), df_o.reshape(n_s, DF_WIDTH), ml_h, state

        xp, p, state = _token_stage(xp, mods_p, attend_prompt, lam_init, weights, tm=TOKEN_TILE, tiles_per_row=seq // TOKEN_TILE)
        conv_p = p["ml_qk"].reshape(bsz, seq, 2 * ML_WIDTH)[:, seq - (ML_CONV - 1):]
        new_prompt.append((p["sb_k"].reshape(bsz, seq, SB_HEADS, SB_DH), p["sb_v"].reshape(bsz, seq, SB_HEADS, SB_DH),
                           p["df_k"].reshape(bsz, seq, DF_KV_HEADS, 2 * DF_DH), p["df_v"].reshape(bsz, seq, DF_KV_HEADS, DF_VDIM),
                           *state, conv_p))
        xs, p, state = _token_stage(xs, mods_s, attend_sample, lam_init, weights, tm=n_s, tiles_per_row=1)
        window = jnp.concatenate([state_ml_conv[l], p["ml_qk"].reshape(nb, ts, 2 * ML_WIDTH)], axis=1)
        new_sample.append((p["sb_k"].reshape(nb, ts, SB_HEADS, SB_DH), p["sb_v"].reshape(nb, ts, SB_HEADS, SB_DH),
                           p["df_k"].reshape(nb, ts, DF_KV_HEADS, 2 * DF_DH), p["df_v"].reshape(nb, ts, DF_KV_HEADS, DF_VDIM),
                           *state, window[:, ts:]))
    y_prompt = final_norm(xp, g_final, tm=TOKEN_TILE).reshape(bsz, seq, d)
    y_sample = final_norm(xs, g_final, tm=n_s).reshape(nb, ts, d)
    stack = lambda group: [jnp.stack(z) for z in zip(*group)]
    return (y_prompt, y_sample, *stack(new_prompt), *stack(new_sample))
```

```python
import functools
import math

import jax
import jax.numpy as jnp
from jax import lax
from jax.experimental import pallas as pl
from jax.experimental.pallas import tpu as pltpu

F32 = jnp.float32
BF16 = jnp.bfloat16

D_MODEL = 1024
SB_HEADS, SB_DH = 8, 64
SB_WIDTH = SB_HEADS * SB_DH
DF_HEADS, DF_KV_HEADS, DF_DH = 8, 4, 64
DF_VDIM = 2 * DF_DH
DF_WIDTH = DF_HEADS * DF_VDIM
ML_HEADS, ML_DH = 4, 128
ML_WIDTH = ML_HEADS * ML_DH
ML_CONV = 4
ML_CHUNK = 128
PK_HEADS, PK_NKEYS, PK_QDIM, PK_TOPK = 8, 128, 256, 16
PK_HALF = PK_QDIM // 2
PK_EXPERTS = PK_NKEYS * PK_NKEYS
EPS = 1e-6
NEG = -1e30
LOG2E = math.log2(math.e)

LANES = 128
SUBLANES = 8
VMEM_LIMIT = 56 * 1024 * 1024

_OFF_ML_I = 3 * SB_WIDTH + DF_HEADS * 2 * DF_DH + DF_KV_HEADS * 2 * DF_DH + DF_KV_HEADS * DF_VDIM + 2 * ML_WIDTH + 2 * ML_WIDTH
_OFF_GATES = _OFF_ML_I + 2 * ML_HEADS
_MAIN_OUTS = (("sb_q", SB_WIDTH), ("sb_k", SB_WIDTH), ("sb_v", SB_WIDTH), ("df_q", DF_HEADS * 2 * DF_DH),
              ("df_k", DF_KV_HEADS * 2 * DF_DH), ("df_v", DF_KV_HEADS * DF_VDIM), ("ml_qk", 2 * ML_WIDTH),
              ("ml_v", ML_WIDTH), ("ml_o", ML_WIDTH), ("gates", 3 * D_MODEL))
_MAIN_WIDTH = sum(w for _, w in _MAIN_OUTS)


def _params(*sem):
    return pltpu.CompilerParams(dimension_semantics=sem, vmem_limit_bytes=VMEM_LIMIT)


def _softplus(z):
    return jnp.maximum(z, 0.0) + jnp.log1p(jnp.exp(-jnp.abs(z)))


def _sigmoid(z):
    return 1.0 / (1.0 + jnp.exp(-z))


def _rms(x):
    return x * lax.rsqrt(jnp.mean(x * x, axis=-1, keepdims=True) + EPS)


def _modulate(x, g, shift, scale):
    return (_rms(x) * g) * (1.0 + scale) + shift


def _mod_spec(mod, tm, tiles_per_row):
    return pl.BlockSpec((None, mod.shape[1], mod.shape[2]), lambda i: (i // tiles_per_row, 0, 0))


def _ada_kernel(c_ref, w_ref, b_ref, o_ref):
    c = c_ref[...]
    a = c * _sigmoid(c)
    o_ref[...] = jnp.dot(a, w_ref[...], preferred_element_type=F32, precision=lax.Precision.HIGHEST) + b_ref[...]


def adaln_all(c_all, w_ada, b_ada):
    depth, d, d6 = w_ada.shape
    r = c_all.shape[0]
    tn = 1024
    return pl.pallas_call(
        _ada_kernel,
        grid=(depth, d6 // tn),
        in_specs=[pl.BlockSpec((r, d), lambda l, j: (0, 0)),
                  pl.BlockSpec((None, d, tn), lambda l, j: (l, 0, j)),
                  pl.BlockSpec((None, 1, tn), lambda l, j: (l, 0, j))],
        out_specs=pl.BlockSpec((None, r, tn), lambda l, j: (l, 0, j)),
        out_shape=jax.ShapeDtypeStruct((depth, r, d6), F32),
        compiler_params=_params("arbitrary", "arbitrary"),
        name="adaln",
    )(c_all, w_ada, b_ada.reshape(depth, 1, d6))


def _inproj_kernel(x_ref, g_ref, sh_ref, sc_ref, wm_ref, wif_ref, *out_refs):
    h = _modulate(x_ref[...], g_ref[...], sh_ref[...], sc_ref[...])
    hb = h.astype(BF16)
    off = 0
    for (_, width), o_ref in zip(_MAIN_OUTS, out_refs[:-1]):
        o_ref[...] = jnp.dot(hb, wm_ref[:, off:off + width], preferred_element_type=F32)
        off += width
    out_refs[-1][...] = jnp.dot(h, wif_ref[...], preferred_element_type=F32, precision=lax.Precision.HIGHEST)


def prep_inproj_weights(w_in_l):
    w_main = jnp.concatenate([w_in_l[:, :_OFF_ML_I], w_in_l[:, _OFF_GATES:]], axis=1).astype(BF16)
    w_if = jnp.pad(w_in_l[:, _OFF_ML_I:_OFF_GATES], ((0, 0), (0, LANES - 2 * ML_HEADS)))
    return w_main, w_if


def inproj(x, g, shift, scale, w_main, w_if, *, tm, tiles_per_row):
    n, d = x.shape
    row = lambda width: pl.BlockSpec((tm, width), lambda i: (i, 0))
    const = lambda shape: pl.BlockSpec(shape, lambda i: (0,) * len(shape), pipeline_mode=pl.Buffered(1))
    outs = pl.pallas_call(
        _inproj_kernel,
        grid=(n // tm,),
        in_specs=[row(d), const((1, d)), _mod_spec(shift, tm, tiles_per_row), _mod_spec(scale, tm, tiles_per_row),
                  const(w_main.shape), const(w_if.shape)],
        out_specs=[row(w) for _, w in _MAIN_OUTS] + [row(LANES)],
        out_shape=[jax.ShapeDtypeStruct((n, w), F32) for _, w in _MAIN_OUTS] + [jax.ShapeDtypeStruct((n, LANES), F32)],
        compiler_params=_params("arbitrary"),
        name="inproj",
    )(x, g.reshape(1, d), shift, scale, w_main, w_if)
    res = {name: o for (name, _), o in zip(_MAIN_OUTS, outs[:-1])}
    res["ml_if"] = outs[-1]
    return res


def _merge_kernel(x_ref, ga_ref, sb_ref, df_ref, mlh_ref, mlo_ref, gt_ref, gdf_ref, gml_ref,
                  wsb_ref, wdf_ref, wml_ref, wout_ref, o_ref, *, lam_init):
    d = x_ref.shape[-1]
    sb = jnp.dot(sb_ref[...].astype(BF16), wsb_ref[...], preferred_element_type=F32)
    gdf = gdf_ref[...] * (1.0 - lam_init)
    dfn = [(_rms(df_ref[:, h * DF_VDIM:(h + 1) * DF_VDIM]) * gdf).astype(BF16) for h in range(DF_HEADS)]
    df = jnp.dot(jnp.concatenate(dfn, axis=1), wdf_ref[...], preferred_element_type=F32)
    mln = []
    for h in range(ML_HEADS):
        s = slice(h * ML_DH, (h + 1) * ML_DH)
        mln.append((_rms(mlh_ref[:, s]) * gml_ref[:, s] * _sigmoid(mlo_ref[:, s])).astype(BF16))
    ml = jnp.dot(jnp.concatenate(mln, axis=1), wml_ref[...], preferred_element_type=F32)
    merged = (_sigmoid(gt_ref[:, 0:d]) * sb + _sigmoid(gt_ref[:, d:2 * d]) * df
              + _sigmoid(gt_ref[:, 2 * d:3 * d]) * ml)
    y = jnp.dot(merged.astype(BF16), wout_ref[...], preferred_element_type=F32)
    o_ref[...] = x_ref[...] + ga_ref[...] * y


def merge(x, ga, sb_o, df_o, ml_h, ml_o, gates, g_diff, g_ml, wsb, wdf, wml, wout, *, lam_init, tm, tiles_per_row):
    n, d = x.shape
    row = lambda width: pl.BlockSpec((tm, width), lambda i: (i, 0))
    const = lambda shape: pl.BlockSpec(shape, lambda i: (0,) * len(shape))
    return pl.pallas_call(
        functools.partial(_merge_kernel, lam_init=lam_init),
        grid=(n // tm,),
        in_specs=[row(d), _mod_spec(ga, tm, tiles_per_row), row(SB_WIDTH), row(DF_WIDTH), row(ML_WIDTH), row(ML_WIDTH),
                  row(3 * d), const((1, DF_VDIM)), const((1, ML_WIDTH)),
                  const(wsb.shape), const(wdf.shape), const(wml.shape), const(wout.shape)],
        out_specs=row(d),
        out_shape=jax.ShapeDtypeStruct((n, d), F32),
        compiler_params=_params("arbitrary"),
        name="merge",
    )(x, ga, sb_o, df_o, ml_h, ml_o, gates, g_diff.reshape(1, DF_VDIM), g_ml.reshape(1, ML_WIDTH), wsb, wdf, wml, wout)


def _final_norm_kernel(x_ref, g_ref, o_ref):
    o_ref[...] = _rms(x_ref[...]) * g_ref[...]


def final_norm(x, g, *, tm):
    n, d = x.shape
    return pl.pallas_call(
        _final_norm_kernel,
        grid=(n // tm,),
        in_specs=[pl.BlockSpec((tm, d), lambda i: (i, 0)), pl.BlockSpec((1, d), lambda i: (0, 0))],
        out_specs=pl.BlockSpec((tm, d), lambda i: (i, 0)),
        out_shape=jax.ShapeDtypeStruct((n, d), F32),
        compiler_params=_params("arbitrary"),
        name="final_norm",
    )(x, g.reshape(1, d))


def _split_bf16(x):
    hi = x.astype(BF16)
    return hi, (x - hi.astype(F32)).astype(BF16)


def _sb_kernel(q_ref, k_ref, v_ref, o_ref, *, tq, tk):
    qi = pl.program_id(2)
    lane = lax.broadcasted_iota(jnp.int32, (tq, LANES), 1)
    q = q_ref[...] * (SB_DH ** -0.5)
    diff = lax.broadcasted_iota(jnp.int32, (tq, tk), 1) - lax.broadcasted_iota(jnp.int32, (tq, tk), 0)
    sr = lax.broadcasted_iota(jnp.int32, (tk, tk), 0)
    sc = lax.broadcasted_iota(jnp.int32, (tk, tk), 1)
    suffix = (sr > sc).astype(BF16)
    nt = (((1,), (1,)), ((), ()))
    per_slab = LANES // SB_DH
    n_slabs = q_ref.shape[1] // LANES
    n_heads = per_slab * n_slabs
    q_heads = [(sl, jnp.where(lane // SB_DH == h, q[:, sl * LANES:(sl + 1) * LANES], 0.0).astype(BF16))
               for sl in range(n_slabs) for h in range(per_slab)]

    def tile(j, carry, masked):
        start = pl.multiple_of(j * tk, tk)
        kts = [k_ref[pl.ds(start, tk), sl * LANES:(sl + 1) * LANES].astype(BF16) for sl in range(n_slabs)]
        vts = [v_ref[pl.ds(start, tk), sl * LANES:(sl + 1) * LANES].astype(BF16) for sl in range(n_slabs)]
        earlier = diff < qi * tq - j * tk
        out = []
        for (sl, qh), (acc, run) in zip(q_heads, carry):
            kt, vt = kts[sl], vts[sl]
            z2 = lax.dot_general(qh, kt, nt, preferred_element_type=F32) * LOG2E
            nz2 = -z2
            soft = jnp.log2(1.0 + jnp.exp2(jnp.minimum(z2, nz2)))
            log_keep = jnp.minimum(nz2, 0.0) - soft
            if masked:
                log_keep = jnp.where(earlier, log_keep, 0.0)
            within = jnp.dot(log_keep.astype(BF16), suffix, preferred_element_type=F32)
            w = jnp.exp2((z2 + log_keep) + within + run)
            if masked:
                w = jnp.where(earlier, w, 0.0)
            out.append((acc + jnp.dot(w.astype(BF16), vt, preferred_element_type=F32),
                        run + jnp.sum(log_keep, axis=-1, keepdims=True)))
        return tuple(out)

    carry = ((jnp.zeros((tq, LANES), F32), jnp.zeros((tq, 1), F32)),) * n_heads
    first = (qi * tq) // tk
    for r in reversed(range(tq // tk)):
        carry = tile(first + r, carry, True)
    carry = lax.fori_loop(0, first, lambda i, c: tile(first - 1 - i, c, False), carry)
    for sl in range(n_slabs):
        o_ref[:, sl * LANES:(sl + 1) * LANES] = jnp.where(lane < SB_DH, carry[per_slab * sl][0], carry[per_slab * sl + 1][0])


def sb_attention(q, k, v, *, batch, tq, tk, slabs=4):
    n, width = q.shape
    t = n // batch
    nq = t // tq
    kv = lambda a: a.reshape(batch, t, width)
    kv_spec = pl.BlockSpec((None, t, slabs * LANES), lambda b, hp, i: (b, 0, hp))
    q_spec = pl.BlockSpec((tq, slabs * LANES), lambda b, hp, i: (b * nq + i, hp))
    return pl.pallas_call(
        functools.partial(_sb_kernel, tq=tq, tk=tk),
        grid=(batch, width // (slabs * LANES), nq),
        in_specs=[q_spec, kv_spec, kv_spec],
        out_specs=q_spec,
        out_shape=jax.ShapeDtypeStruct((n, width), F32),
        compiler_params=_params("arbitrary", "arbitrary", "arbitrary"),
        name="sb_attention",
    )(q, kv(k), kv(v))


def _lambda_value(lam_ref, lam_init):
    lv = lam_ref[...]
    s1 = jnp.sum(lv[0:1, :] * lv[1:2, :], axis=-1, keepdims=True)
    s2 = jnp.sum(lv[2:3, :] * lv[3:4, :], axis=-1, keepdims=True)
    return jnp.exp(s1) - jnp.exp(s2) + lam_init


def _df_kernel(slope_ref, q_ref, k_ref, v_ref, lam_ref, o_ref, *, tq, tk, lam_init):
    group = DF_HEADS // DF_KV_HEADS
    kvh = pl.program_id(1)
    qi = pl.program_id(2)
    lane = lax.broadcasted_iota(jnp.int32, (tq, LANES), 1)
    row = lax.broadcasted_iota(jnp.int32, (tq, tk), 0)
    col = lax.broadcasted_iota(jnp.int32, (tq, tk), 1)
    diff = (col - row).astype(F32)
    ones = jnp.ones((tk, LANES), BF16)
    nt = (((1,), (1,)), ((), ()))
    chains = []
    for g in range(group):
        slope = slope_ref[kvh * group + g]
        qg = q_ref[:, g * LANES:(g + 1) * LANES] * (DF_DH ** -0.5)
        rel = diff * slope
        for m in range(2):
            chains.append((slope, rel, jnp.where(lane // DF_DH == m, qg, 0.0).astype(BF16)))

    def tile(j, carry, masked):
        start = pl.multiple_of(j * tk, tk)
        kt = k_ref[pl.ds(start, tk), :].astype(BF16)
        vt = jnp.concatenate([v_ref[pl.ds(start, tk), :].astype(BF16), ones], axis=1)
        shift = (qi * tq - j * tk).astype(F32)
        out = []
        for (slope, rel, qm), (acc, mx) in zip(chains, carry):
            s = lax.dot_general(qm, kt, nt, preferred_element_type=F32) + rel
            if masked:
                s = jnp.where(diff <= shift, s, NEG)
            off = -slope * shift
            mx_new = jnp.maximum(mx, jnp.max(s, axis=-1, keepdims=True) + off)
            e = jnp.exp(s + (off - mx_new))
            out.append((jnp.exp(mx - mx_new) * acc + jnp.dot(e.astype(BF16), vt, preferred_element_type=F32), mx_new))
        return tuple(out)

    init = tuple((jnp.zeros((tq, 2 * LANES), F32), jnp.full((tq, 1), NEG, F32)) for _ in chains)
    n_full = (qi * tq) // tk
    carry = lax.fori_loop(0, n_full, lambda j, c: tile(j, c, False), init)
    carry = tile(n_full, carry, True)
    lam = _lambda_value(lam_ref, lam_init)
    for g in range(group):
        res = [carry[2 * g + m][0][:, :LANES] / carry[2 * g + m][0][:, LANES:] for m in range(2)]
        o_ref[:, g * LANES:(g + 1) * LANES] = res[0] - lam * res[1]


def alibi_slope_table():
    return jnp.array([2.0 ** (-8.0 * (h + 1) / DF_HEADS) for h in range(DF_HEADS)], dtype=F32)


def df_attention(q, k, v, lam_vecs, *, lam_init, batch, tq, tk):
    n, width = q.shape
    t = n // batch
    nq = t // tq
    kv = lambda a: a.reshape(batch, t, a.shape[-1])
    group = DF_HEADS // DF_KV_HEADS
    kv_spec = pl.BlockSpec((None, t, LANES), lambda b, h, i, s: (b, 0, h))
    q_spec = pl.BlockSpec((tq, group * LANES), lambda b, h, i, s: (b * nq + i, h))
    return pl.pallas_call(
        functools.partial(_df_kernel, tq=tq, tk=tk, lam_init=lam_init),
        grid_spec=pltpu.PrefetchScalarGridSpec(
            num_scalar_prefetch=1,
            grid=(batch, DF_KV_HEADS, nq),
            in_specs=[q_spec, kv_spec, kv_spec, pl.BlockSpec(lam_vecs.shape, lambda b, h, i, s: (0, 0))],
            out_specs=q_spec),
        out_shape=jax.ShapeDtypeStruct((n, width), F32),
        compiler_params=_params("arbitrary", "arbitrary", "arbitrary"),
        name="df_attention",
    )(alibi_slope_table(), q, kv(k), kv(v), lam_vecs)


def _mlstm_kernel(qk_ref, v_ref, if_ref, conv0_ref, c0_ref, n0_ref, m0_ref, wc_ref, bc_ref, bif_ref,
                  h_ref, c_ref, n_ref, m_ref, buf, cs, ns, ms, *, valid):
    L = ML_CHUNK
    ci = pl.program_id(1)

    @pl.when(ci == 0)
    def _():
        buf[0:SUBLANES, :] = conv0_ref[...]
        cs[...] = c0_ref[...]
        ns[...] = n0_ref[...]
        ms[...] = m0_ref[...]

    cur = qk_ref[...]
    buf[SUBLANES:SUBLANES + L, :] = cur
    acc = bc_ref[...]
    for j in range(ML_CONV):
        lo = SUBLANES - (ML_CONV - 1) + j
        acc = acc + buf[lo:lo + L, :] * wc_ref[j:j + 1, :]
    qk = acc * _sigmoid(acc)
    buf[0:SUBLANES, :] = cur[L - SUBLANES:L, :]

    row = lax.broadcasted_iota(jnp.int32, (L, LANES), 0)
    lane = lax.broadcasted_iota(jnp.int32, (L, LANES), 1)
    gt = if_ref[...] + bif_ref[...]
    is_valid = row < valid
    ig = jnp.where(is_valid, gt, NEG)
    lf = jnp.where(is_valid & (lane >= ML_HEADS) & (lane < 2 * ML_HEADS), -_softplus(-gt), 0.0)
    r2 = lax.broadcasted_iota(jnp.int32, (L, L), 0)
    c2 = lax.broadcasted_iota(jnp.int32, (L, L), 1)
    causal = c2 <= r2
    bcum = jnp.dot(causal.astype(F32), lf, preferred_element_type=F32, precision=lax.Precision.HIGHEST)
    ig_t = ig.T
    bcum_t = bcum.T
    nt = (((1,), (1,)), ((), ()))
    for h in range(ML_HEADS):
        b_col = bcum[:, ML_HEADS + h:ML_HEADS + h + 1]
        b_row = bcum_t[ML_HEADS + h:ML_HEADS + h + 1, :]
        ig_row = ig_t[h:h + 1, :]
        ig_col = ig[:, h:h + 1]
        m_run = ms[h:h + 1, 0:1]
        log_d = jnp.where(causal, b_col - b_row + ig_row, NEG)
        log_inter = b_col + m_run
        m_t = jnp.maximum(log_inter, jnp.max(log_d, axis=-1, keepdims=True))
        a_inter = jnp.exp(log_inter - m_t)
        dmat = jnp.exp(log_d - m_t)
        hs = slice(h * ML_DH, (h + 1) * ML_DH)
        qh = qk[:, hs]
        kh = qk[:, ML_WIDTH + h * ML_DH:ML_WIDTH + (h + 1) * ML_DH] * (ML_DH ** -0.5)
        qb, kb, vb = qh.astype(BF16), kh.astype(BF16), v_ref[:, hs].astype(BF16)
        scores = lax.dot_general(qb, kb, nt, preferred_element_type=F32) * dmat
        c_mat = cs[h]
        n_vec = ns[h:h + 1, :]
        num = (a_inter * jnp.dot(qb, c_mat.astype(BF16), preferred_element_type=F32)
               + jnp.dot(scores.astype(BF16), vb, preferred_element_type=F32))
        den = a_inter * jnp.sum(qh * n_vec, axis=-1, keepdims=True) + jnp.sum(scores, axis=-1, keepdims=True)
        h_ref[:, hs] = num / jnp.maximum(jnp.abs(den), jnp.exp(-m_t))
        m_new = m_t[L - 1:L, :]
        b_last = b_col[L - 1:L, :]
        w_end = jnp.exp(b_last - b_col + ig_col - m_new)
        decay = jnp.exp(b_last + m_run - m_new)
        kw = kh * w_end
        cs[h] = decay * c_mat + jnp.dot(kw.T.astype(BF16), vb, preferred_element_type=F32)
        ns[h:h + 1, :] = decay * n_vec + jnp.sum(kw, axis=0, keepdims=True)
        ms[h:h + 1, :] = jnp.broadcast_to(m_new, (1, LANES))

    @pl.when(ci == pl.num_programs(1) - 1)
    def _():
        c_ref[...] = cs[...]
        n_ref[...] = ns[...]
        m_ref[...] = ms[...]


def mlstm(ml_qk, ml_v, ml_if, conv8, c0, n0, m0, w_conv, b_conv, b_ig, b_fg, *, batch, valid):
    n_tok = ml_qk.shape[0]
    L = ML_CHUNK
    nc = n_tok // batch // L
    pad_rows = lambda a: jnp.pad(a, ((0, 0), (0, SUBLANES - ML_HEADS), (0, 0)))
    n0p = pad_rows(n0)
    m0p = pad_rows(jnp.broadcast_to(m0[:, :, None], (batch, ML_HEADS, LANES)))
    bias_if = jnp.pad(jnp.concatenate([b_ig, b_fg]), (0, LANES - 2 * ML_HEADS)).reshape(1, LANES)
    tok = lambda width: pl.BlockSpec((L, width), lambda b, c: (b * nc + c, 0))
    per_b = lambda *shape: pl.BlockSpec((None,) + shape, lambda b, c: (b,) + (0,) * len(shape))
    const = lambda *shape: pl.BlockSpec(shape, lambda b, c: (0,) * len(shape))
    h, c_out, n_out, m_out = pl.pallas_call(
        functools.partial(_mlstm_kernel, valid=valid),
        grid=(batch, nc),
        in_specs=[tok(2 * ML_WIDTH), tok(ML_WIDTH), tok(LANES), per_b(SUBLANES, 2 * ML_WIDTH),
                  per_b(ML_HEADS, ML_DH, ML_DH), per_b(SUBLANES, LANES), per_b(SUBLANES, LANES),
                  const(ML_CONV, 2 * ML_WIDTH), const(1, 2 * ML_WIDTH), const(1, LANES)],
        out_specs=[tok(ML_WIDTH), per_b(ML_HEADS, ML_DH, ML_DH), per_b(SUBLANES, LANES), per_b(SUBLANES, LANES)],
        out_shape=[jax.ShapeDtypeStruct((n_tok, ML_WIDTH), F32), jax.ShapeDtypeStruct((batch, ML_HEADS, ML_DH, ML_DH), F32),
                   jax.ShapeDtypeStruct((batch, SUBLANES, LANES), F32), jax.ShapeDtypeStruct((batch, SUBLANES, LANES), F32)],
        scratch_shapes=[pltpu.VMEM((SUBLANES + L, 2 * ML_WIDTH), F32), pltpu.VMEM((ML_HEADS, ML_DH, ML_DH), F32),
                        pltpu.VMEM((SUBLANES, LANES), F32), pltpu.VMEM((SUBLANES, LANES), F32)],
        compiler_params=_params("arbitrary", "arbitrary"),
        name="mlstm",
    )(ml_qk, ml_v, ml_if, conv8, c0, n0p, m0p, w_conv, b_conv.reshape(1, -1), bias_if)
    return h, c_out, n_out[:, :ML_HEADS], m_out[:, :ML_HEADS, 0]


def _top16(s):
    rows = lax.broadcasted_iota(jnp.int32, (PK_TOPK, s.shape[1]), 0)
    work = s
    rank = jnp.full(s.shape, float(PK_TOPK), F32)
    top = jnp.zeros((PK_TOPK, s.shape[1]), F32)
    for k in range(PK_TOPK):
        m = jnp.max(work, axis=0, keepdims=True)
        sel = work == m
        rank = jnp.where(sel, float(k), rank)
        work = jnp.where(sel, NEG, work)
        top = jnp.where(rows == k, m, top)
    return top, rank


def _route_kernel(x_ref, g_ref, sh_ref, sc_ref, wpq_ref, keys_ref, xt_ref, r2_ref, nb_ref, e1_ref, e2_ref):
    tm = x_ref.shape[0]
    x2 = _modulate(x_ref[...], g_ref[...], sh_ref[...], sc_ref[...])
    xt_ref[...] = x2.T.astype(BF16)
    q = jnp.dot(x2.astype(BF16), wpq_ref[...], preferred_element_type=F32)
    nt = (((1,), (1,)), ((), ()))
    row8 = lax.broadcasted_iota(jnp.int32, (SUBLANES, tm), 0)
    for h in range(PK_HEADS):
        s, top, rank = [], [], []
        for p in range(2):
            lo = (2 * h + p) * PK_HALF
            sp = lax.dot_general(keys_ref[h, p], q[:, lo:lo + PK_HALF].astype(BF16), nt, preferred_element_type=F32)
            tp, rk = _top16(sp)
            s.append(sp)
            top.append(tp)
            rank.append(rk)
        blocks = [top[0][0:1, :] + top[1]]
        for a in range(1, PK_TOPK):
            width = PK_TOPK // (a + 1)
            blocks.append(jnp.where(row8 < width, top[0][a:a + 1, :] + top[1][0:SUBLANES, :], NEG))
        cand = jnp.concatenate(blocks, axis=0)
        work = cand
        for k in range(PK_TOPK):
            thr = jnp.max(work, axis=0, keepdims=True)
            work = jnp.where(work == thr, NEG, work)
        chosen = cand >= thr
        cmax = top[0][0:1, :] + top[1][0:1, :]
        z = jnp.sum(jnp.where(chosen, jnp.exp(cand - cmax), 0.0), axis=0, keepdims=True)
        chosen_f = chosen.astype(F32)
        nb_rows = jnp.zeros_like(rank[0])
        lo = 0
        for a in range(PK_TOPK):
            size = PK_TOPK if a == 0 else SUBLANES
            nb_a = jnp.sum(chosen_f[lo:lo + size, :], axis=0, keepdims=True)
            nb_rows = jnp.where(rank[0] == float(a), nb_a, nb_rows)
            lo += size
        r2_ref[h] = rank[1].astype(BF16)
        nb_ref[h] = nb_rows
        e1_ref[h] = jnp.exp(s[0] - top[0][0:1, :]) / z
        e2_ref[h] = jnp.exp(s[1] - top[1][0:1, :]).astype(BF16)


def peer_route(x, g, shift, scale, w_pq, sub_keys, *, tm, tiles_per_row):
    n, d = x.shape
    const = lambda shape: pl.BlockSpec(shape, lambda i: (0,) * len(shape))
    lanes = pl.BlockSpec((PK_HEADS, PK_NKEYS, tm), lambda i: (0, 0, i))
    routed = lambda dtype: jax.ShapeDtypeStruct((PK_HEADS, PK_NKEYS, n), dtype)
    return pl.pallas_call(
        _route_kernel,
        grid=(n // tm,),
        in_specs=[pl.BlockSpec((tm, d), lambda i: (i, 0)), const((1, d)), _mod_spec(shift, tm, tiles_per_row),
                  _mod_spec(scale, tm, tiles_per_row), const(w_pq.shape), const(sub_keys.shape)],
        out_specs=[pl.BlockSpec((d, tm), lambda i: (0, i)), lanes, lanes, lanes, lanes],
        out_shape=[jax.ShapeDtypeStruct((d, n), BF16), routed(BF16), routed(F32), routed(F32), routed(BF16)],
        compiler_params=_params("arbitrary"),
        name="peer_route",
    )(x, g.reshape(1, d), shift, scale, w_pq, sub_keys)


def _gelu_tanh(x):
    return 0.5 * x * (1.0 + jnp.tanh(math.sqrt(2.0 / math.pi) * (x + 0.044715 * (x * x * x))))


BF16_ROWS = 16


def _expert_kernel(x_ref, ga_ref, xt_ref, r2_ref, nb_ref, e1_ref, e2_ref, u_ref, vt_ref, o_ref, acc, *, sub):
    c = pl.program_id(1)
    tm = xt_ref.shape[1]
    groups = PK_NKEYS // BF16_ROWS

    @pl.when(c == 0)
    def _():
        acc[...] = jnp.zeros_like(acc)

    xt = xt_ref[...]
    rank2 = [r2_ref[h].reshape(groups, BF16_ROWS, tm) for h in range(PK_HEADS)]
    gate2 = [e2_ref[h].reshape(groups, BF16_ROWS, tm) for h in range(PK_HEADS)]
    tile_row = lambda ref, h, e: jnp.broadcast_to(ref[h, e:e + 1, :], (BF16_ROWS, tm)).astype(BF16)[None]
    for s0 in range(0, u_ref.shape[0], sub):
        act = _gelu_tanh(jnp.dot(u_ref[s0:s0 + sub, :], xt, preferred_element_type=F32).astype(BF16))
        slabs = []
        for e in range(s0 // PK_NKEYS, (s0 + sub) // PK_NKEYS):
            gate = None
            for h in range(PK_HEADS):
                term = jnp.where(rank2[h] < tile_row(nb_ref, h, e), gate2[h], jnp.zeros_like(gate2[h])) * tile_row(e1_ref, h, e)
                gate = term if gate is None else gate + term
            slabs.append(gate.reshape(PK_NKEYS, tm))
        weighted = jnp.concatenate(slabs, axis=0) * act
        acc[...] += jnp.dot(vt_ref[:, s0:s0 + sub], weighted, preferred_element_type=F32)

    @pl.when(c == pl.num_programs(1) - 1)
    def _():
        o_ref[...] = x_ref[...] + ga_ref[...] * acc[...].T


def peer_expert(x, ga, xt, r2, nb, e1c, e2, u, vt, *, tm, tiles_per_row, chunk=2048, sub=2048):
    n, d = x.shape
    n_exp = u.shape[0]
    e1_per_chunk = chunk // PK_NKEYS
    lanes = pl.BlockSpec((PK_HEADS, PK_NKEYS, tm), lambda i, c: (0, 0, i))
    by_e1 = pl.BlockSpec((PK_HEADS, e1_per_chunk, tm), lambda i, c: (0, c, i))
    mod = pl.BlockSpec((None, ga.shape[1], ga.shape[2]), lambda i, c: (i // tiles_per_row, 0, 0))
    return pl.pallas_call(
        functools.partial(_expert_kernel, sub=sub),
        grid=(n // tm, n_exp // chunk),
        in_specs=[pl.BlockSpec((tm, d), lambda i, c: (i, 0)), mod, pl.BlockSpec((d, tm), lambda i, c: (0, i)),
                  lanes, by_e1, by_e1, lanes,
                  pl.BlockSpec((chunk, d), lambda i, c: (c, 0)), pl.BlockSpec((d, chunk), lambda i, c: (0, c))],
        out_specs=pl.BlockSpec((tm, d), lambda i, c: (i, 0)),
        out_shape=jax.ShapeDtypeStruct((n, d), F32),
        scratch_shapes=[pltpu.VMEM((d, tm), F32)],
        compiler_params=_params("arbitrary", "arbitrary"),
        name="peer_expert",
    )(x, ga, xt, r2, nb, e1c, e2, u, vt)


PAGES_PER_STEP = 16


def _page_specs(n_pages, rows, width, layer, pps):
    return [pl.BlockSpec((None, None, rows, width),
                         lambda b, s, pt, k=k: (layer, pt[b, n_pages - 1 - (s * pps + k)], 0, 0)) for k in range(pps)]


def _sbs_kernel(pt_ref, qb_ref, kn_ref, vn_ref, sfx_ref, *refs, pps):
    k_refs, v_refs, o_ref, acc, run = refs[:pps], refs[pps:2 * pps], refs[2 * pps], refs[2 * pps + 1], refs[2 * pps + 2]
    s = pl.program_id(1)
    n_q, page = run.shape
    t_new = o_ref.shape[0]
    qb = (qb_ref[...] * (SB_DH ** -0.5)).astype(BF16)
    nt = (((1,), (1,)), ((), ()))

    def process(pages, mask):
        zs = [jnp.dot(qb, k_ref[...].astype(BF16), preferred_element_type=F32) for k_ref, _ in pages]
        sps = [_softplus(z) for z in zs]
        log_keeps = [-sp if mask is None else jnp.where(mask, -sp, 0.0) for sp in sps]
        hi, lo = _split_bf16(jnp.concatenate(log_keeps, axis=0))
        suffix = sfx_ref[...]
        within = jnp.dot(hi, suffix, preferred_element_type=F32) + jnp.dot(lo, suffix, preferred_element_type=F32)
        running = run[:, 0:1]
        out = acc[...]
        for i, (z, sp, log_keep, (_, v_ref)) in enumerate(zip(zs, sps, log_keeps, pages)):
            w = jnp.exp((z - sp) + within[i * n_q:(i + 1) * n_q, :] + running)
            if mask is not None:
                w = jnp.where(mask, w, 0.0)
            out = out + lax.dot_general(w.astype(BF16), v_ref[...].astype(BF16), nt, preferred_element_type=F32)
            running = running + jnp.sum(log_keep, axis=1, keepdims=True)
        acc[...] = out
        run[...] = jnp.broadcast_to(running, run.shape)

    @pl.when(s == 0)
    def _():
        acc[...] = jnp.zeros_like(acc)
        run[...] = jnp.zeros_like(run)
        pos = lax.broadcasted_iota(jnp.int32, (n_q, page), 1)
        t_row = lax.broadcasted_iota(jnp.int32, (n_q, page), 0) // SB_HEADS
        process([(kn_ref, vn_ref)], pos < t_row)

    process(list(zip(k_refs, v_refs)), None)

    @pl.when(s == pl.num_programs(1) - 1)
    def _():
        h_row = lax.broadcasted_iota(jnp.int32, (SB_HEADS, SB_WIDTH), 0)
        h_col = lax.broadcasted_iota(jnp.int32, (SB_HEADS, SB_WIDTH), 1) // SB_DH
        own = acc[...].reshape(t_new, SB_HEADS, SB_WIDTH) * (h_row == h_col).astype(F32)
        o_ref[...] = jnp.sum(own, axis=1)


def sb_attention_sample(page_table, q, k_new, v_new, cache_k, cache_v, *, layer, pps=PAGES_PER_STEP):
    nb, t_new, width = q.shape
    n_pages = page_table.shape[1]
    pps = math.gcd(pps, n_pages)
    depth, n_pool, page, heads, dh = cache_k.shape
    n_q = t_new * heads
    eye = jnp.eye(heads, dtype=F32)[None, None, :, :, None]
    qb = (q.reshape(nb, t_new, heads, 1, dh) * eye).reshape(nb, n_q, width)
    new_page = lambda a: jnp.pad(a.transpose(0, 2, 1), ((0, 0), (0, 0), (0, page - t_new)))
    paged = lambda a: a.transpose(0, 1, 3, 4, 2).reshape(depth, n_pool, width, page)
    per_b = lambda *shape: pl.BlockSpec((None,) + shape, lambda b, s, pt: (b,) + (0,) * len(shape))
    key = jnp.arange(page)
    suffix = (key[:, None] > key[None, :]).astype(BF16)
    return pl.pallas_call(
        functools.partial(_sbs_kernel, pps=pps),
        grid_spec=pltpu.PrefetchScalarGridSpec(
            num_scalar_prefetch=1,
            grid=(nb, n_pages // pps),
            in_specs=[per_b(n_q, width), per_b(width, page), per_b(width, page),
                      pl.BlockSpec(suffix.shape, lambda b, s, pt: (0, 0))]
            + _page_specs(n_pages, width, page, layer, pps) + _page_specs(n_pages, width, page, layer, pps),
            out_specs=per_b(t_new, width),
            scratch_shapes=[pltpu.VMEM((n_q, width), F32), pltpu.VMEM((n_q, page), F32)]),
        out_shape=jax.ShapeDtypeStruct((nb, t_new, width), F32),
        compiler_params=_params("arbitrary", "arbitrary"),
        name="sb_attention_sample",
    )(page_table, qb, new_page(k_new), new_page(v_new), suffix, *([paged(cache_k)] * pps), *([paged(cache_v)] * pps))


def _dfs_kernel(pt_ref, qc_ref, kn_ref, vn_ref, col_ref, lam_ref, *refs, pps, n_pages, lam_init):
    k_refs, v_refs, o_ref, acc, mx = refs[:pps], refs[pps:2 * pps], refs[2 * pps], refs[2 * pps + 1], refs[2 * pps + 2]
    s = pl.program_id(1)
    rows = kn_ref.shape[0]
    page = rows // DF_KV_HEADS
    half = LANES // 4
    r = lax.broadcasted_iota(jnp.int32, (rows, LANES), 0)
    c = lax.broadcasted_iota(jnp.int32, (rows, LANES), 1)
    group = DF_HEADS // DF_KV_HEADS
    same_kv = (r % DF_KV_HEADS == (c // group) % DF_KV_HEADS) & (c < 2 * half)
    slope = col_ref[0:1, :]
    t_col = col_ref[1:2, :]
    tok = (r // DF_KV_HEADS).astype(F32)
    slope_tok = slope * tok
    qc = (qc_ref[...] * (DF_DH ** -0.5)).astype(BF16)

    def process(pages, mask):
        scores = []
        for k_ref, _, base in pages:
            z = jnp.dot(k_ref[...].astype(BF16), qc, preferred_element_type=F32)
            scores.append(jnp.where(mask, z + slope_tok + slope * (base - t_col), NEG))
        mx_old = mx[0:1, :]
        mx_new = mx_old
        for sb in scores:
            mx_new = jnp.maximum(mx_new, jnp.max(sb, axis=0, keepdims=True))
        e = jnp.concatenate([jnp.exp(sb - mx_new).astype(BF16) for sb in scores], axis=0)
        vt = jnp.concatenate([v_ref[...].T.astype(BF16) for _, v_ref, _ in pages], axis=1)
        vext = jnp.concatenate([vt, jnp.ones_like(vt)], axis=0)
        acc[...] = jnp.exp(mx_old - mx_new) * acc[...] + jnp.dot(vext, e, preferred_element_type=F32)
        mx[...] = jnp.broadcast_to(mx_new, mx.shape)

    @pl.when(s == 0)
    def _():
        acc[...] = jnp.zeros_like(acc)
        mx[...] = jnp.full_like(mx, NEG)
        process([(kn_ref, vn_ref, 0.0)], same_kv & (tok <= t_col))

    first_page = n_pages - 1 - s * pps
    process([(k_refs[k], v_refs[k], ((first_page - k - n_pages) * page).astype(F32)) for k in range(pps)], same_kv)

    @pl.when(s == pl.num_programs(1) - 1)
    def _():
        res = acc[0:LANES, :] / acc[LANES:2 * LANES, :]
        o_ref[...] = res - _lambda_value(lam_ref, lam_init) * pltpu.roll(res, LANES - half, axis=1)


def df_attention_sample(page_table, q, k_new, v_new, cache_k, cache_v, lam_vecs, *, layer, lam_init, pps=PAGES_PER_STEP):
    nb, t_new, _ = q.shape
    n_pages = page_table.shape[1]
    pps = math.gcd(pps, n_pages)
    depth, n_pool, page, kvh, width = cache_k.shape
    rows = page * kvh
    group = DF_HEADS // DF_KV_HEADS
    half = LANES // 4
    assert t_new * kvh * group == half and width == LANES
    qt = q.reshape(nb, t_new * kvh * group, width).transpose(0, 2, 1)
    d_half = (jnp.arange(width) // DF_DH)[None, :, None]
    qc = jnp.concatenate([jnp.where(d_half == 0, qt, 0.0), jnp.where(d_half == 1, qt, 0.0),
                          jnp.zeros((nb, width, LANES - 2 * half), F32)], axis=2)
    col = jnp.arange(LANES)
    head_of_col = ((col // group) % kvh) * group + col % group
    slope_col = jnp.where(col < 2 * half, alibi_slope_table()[head_of_col], 0.0)
    t_of_col = ((col % half) // (kvh * group)).astype(F32)
    col_info = jnp.zeros((SUBLANES, LANES), F32).at[0].set(slope_col).at[1].set(t_of_col)
    pad_page = lambda a: jnp.pad(a, ((0, 0), (0, rows - a.shape[1]), (0, 0)))
    paged = lambda a: a.reshape(depth, n_pool, rows, width)
    per_b = lambda *shape: pl.BlockSpec((None,) + shape, lambda b, s, pt: (b,) + (0,) * len(shape))
    const = lambda *shape: pl.BlockSpec(shape, lambda b, s, pt: (0,) * len(shape))
    out = pl.pallas_call(
        functools.partial(_dfs_kernel, pps=pps, n_pages=n_pages, lam_init=lam_init),
        grid_spec=pltpu.PrefetchScalarGridSpec(
            num_scalar_prefetch=1,
            grid=(nb, n_pages // pps),
            in_specs=[per_b(width, LANES), per_b(rows, width), per_b(rows, width), const(SUBLANES, LANES), const(*lam_vecs.shape)]
            + _page_specs(n_pages, rows, width, layer, pps) + _page_specs(n_pages, rows, width, layer, pps),
            out_specs=per_b(width, LANES),
            scratch_shapes=[pltpu.VMEM((2 * LANES, LANES), F32), pltpu.VMEM((SUBLANES, LANES), F32)]),
        out_shape=jax.ShapeDtypeStruct((nb, width, LANES), F32),
        compiler_params=_params("arbitrary", "arbitrary"),
        name="df_attention_sample",
    )(page_table, qc, pad_page(k_new), pad_page(v_new), col_info, lam_vecs, *([paged(cache_k)] * pps), *([paged(cache_v)] * pps))
    return out[:, :, :half].transpose(0, 2, 1).reshape(nb, t_new, kvh * group * width)


TOKEN_TILE = 256
ATTN_TILE = 256


def _token_stage(x, mods, attend, lam_init, weights, *, tm, tiles_per_row):
    (g1, g2, g_diff_l, g_ml_l, w_main, w_if, wsb, wdf, wml, wout, wpq, keys, u, vt) = weights
    p = inproj(x, g1, mods[0], mods[1], w_main, w_if, tm=tm, tiles_per_row=tiles_per_row)
    sb_o, df_o, ml_h, state = attend(p)
    x = merge(x, mods[2], sb_o, df_o, ml_h, p["ml_o"], p["gates"], g_diff_l, g_ml_l, wsb, wdf, wml, wout,
              lam_init=lam_init, tm=tm, tiles_per_row=tiles_per_row)
    routed = peer_route(x, g2, mods[3], mods[4], wpq, keys, tm=tm, tiles_per_row=tiles_per_row)
    wide = 2 if tiles_per_row % 2 == 0 else 1
    x = peer_expert(x, mods[5], *routed, u, vt, tm=wide * tm, tiles_per_row=tiles_per_row // wide)
    return x, p, state


def kernel(x_prompt, x_sample, cache_sb_k, cache_sb_v, cache_df_k, cache_df_v, state_ml_c, state_ml_n, state_ml_m, state_ml_conv, page_table, c_prompt, c_sample, w_ada, b_ada, g_norm1, g_norm2, g_final, w_in, w_conv, b_conv, b_ig, b_fg, g_ml, lam_q1, lam_k1, lam_q2, lam_k2, g_diff, w_br_sb, w_br_df, w_br_ml, w_out, w_pq, sub_keys, peer_u, peer_v):
    bsz, seq, d = x_prompt.shape
    nb, ts, _ = x_sample.shape
    depth = w_ada.shape[0]
    n_p, n_s = bsz * seq, nb * ts
    xp = x_prompt.reshape(n_p, d)
    xs = x_sample.reshape(n_s, d)
    mod = adaln_all(jnp.concatenate([c_prompt, c_sample], axis=0), w_ada, b_ada).reshape(depth, bsz + nb, 6, d)
    new_prompt, new_sample = [], []
    for l in range(depth):
        lam_init = 0.8 - 0.6 * math.exp(-0.3 * l)
        mods_p = [mod[l, :bsz, i][:, None, :] for i in range(6)]
        mods_s = [jnp.repeat(mod[l, bsz:, i], ts, axis=0)[None] for i in range(6)]
        lam_vecs = jnp.stack([lam_q1[l], lam_k1[l], lam_q2[l], lam_k2[l]])
        bf = lambda a: a.astype(BF16)
        weights = (g_norm1[l], g_norm2[l], g_diff[l], g_ml[l], *prep_inproj_weights(w_in[l]), bf(w_br_sb[l]), bf(w_br_df[l]),
                   bf(w_br_ml[l]), bf(w_out[l]), bf(w_pq[l]), bf(sub_keys[l]), bf(peer_u[l]), bf(peer_v[l].T))
        conv_w = (w_conv[l], b_conv[l], b_ig[l], b_fg[l])

        def attend_prompt(p):
            sb_o = sb_attention(p["sb_q"], p["sb_k"], p["sb_v"], batch=bsz, tq=min(2 * ATTN_TILE, seq), tk=ATTN_TILE)
            df_o = df_attention(p["df_q"], p["df_k"], p["df_v"], lam_vecs, lam_init=lam_init, batch=bsz,
                                tq=min(2 * ATTN_TILE, seq), tk=min(2 * ATTN_TILE, seq))
            zeros = lambda *shape: jnp.zeros((bsz,) + shape, F32)
            ml_h, *state = mlstm(p["ml_qk"], p["ml_v"], p["ml_if"], zeros(SUBLANES, 2 * ML_WIDTH), zeros(ML_HEADS, ML_DH, ML_DH),
                                 zeros(ML_HEADS, ML_DH), zeros(ML_HEADS), *conv_w, batch=bsz, valid=ML_CHUNK)
            return sb_o, df_o, ml_h, state

        def attend_sample(p):
            per_seq = lambda a, rows: a.reshape(nb, rows, a.shape[-1] * ts // rows)
            sb_o = sb_attention_sample(page_table, per_seq(p["sb_q"], ts), per_seq(p["sb_k"], ts), per_seq(p["sb_v"], ts),
                                       cache_sb_k, cache_sb_v, layer=l)
            df_o = df_attention_sample(page_table, per_seq(p["df_q"], ts), per_seq(p["df_k"], ts * DF_KV_HEADS),
                                       per_seq(p["df_v"], ts * DF_KV_HEADS), cache_df_k, cache_df_v, lam_vecs,
                                       layer=l, lam_init=lam_init)
            chunked = lambda a: jnp.pad(per_seq(a, ts), ((0, 0), (0, ML_CHUNK - ts), (0, 0))).reshape(nb * ML_CHUNK, a.shape[-1])
            conv8 = jnp.pad(state_ml_conv[l], ((0, 0), (SUBLANES - (ML_CONV - 1), 0), (0, 0)))
            ml_h, *state = mlstm(chunked(p["ml_qk"]), chunked(p["ml_v"]), chunked(p["ml_if"]), conv8, state_ml_c[l], state_ml_n[l],
                                 state_ml_m[l], *conv_w, batch=nb, valid=ts)
            ml_h = ml_h.reshape(nb, ML_CHUNK, ML_WIDTH)[:, :ts].reshape(n_s, ML_WIDTH)
            return sb_o.reshape(n_s, SB_WIDTH), df_o.reshape(n_s, DF_WIDTH), ml_h, state

        xp, p, state = _token_stage(xp, mods_p, attend_prompt, lam_init, weights, tm=TOKEN_TILE, tiles_per_row=seq // TOKEN_TILE)
        conv_p = p["ml_qk"].reshape(bsz, seq, 2 * ML_WIDTH)[:, seq - (ML_CONV - 1):]
        new_prompt.append((p["sb_k"].reshape(bsz, seq, SB_HEADS, SB_DH), p["sb_v"].reshape(bsz, seq, SB_HEADS, SB_DH),
                           p["df_k"].reshape(bsz, seq, DF_KV_HEADS, 2 * DF_DH), p["df_v"].reshape(bsz, seq, DF_KV_HEADS, DF_VDIM),
                           *state, conv_p))
        xs, p, state = _token_stage(xs, mods_s, attend_sample, lam_init, weights, tm=n_s, tiles_per_row=1)
        window = jnp.concatenate([state_ml_conv[l], p["ml_qk"].reshape(nb, ts, 2 * ML_WIDTH)], axis=1)
        new_sample.append((p["sb_k"].reshape(nb, ts, SB_HEADS, SB_DH), p["sb_v"].reshape(nb, ts, SB_HEADS, SB_DH),
                           p["df_k"].reshape(nb, ts, DF_KV_HEADS, 2 * DF_DH), p["df_v"].reshape(nb, ts, DF_KV_HEADS, DF_VDIM),
                           *state, window[:, ts:]))
    y_prompt = final_norm(xp, g_final, tm=TOKEN_TILE).reshape(bsz, seq, d)
    y_sample = final_norm(xs, g_final, tm=n_s).reshape(nb, ts, d)
    stack = lambda group: [jnp.stack(z) for z in zip(*group)]
    return (y_prompt, y_sample, *stack(new_prompt), *stack(new_sample))
```
